```python
import math
import jax, jax.numpy as jnp
from jax import lax
import numpy as np

D_MODEL = 1024
BATCH = 4
SEQ = 8192
DEPTH = 1
DEC_BATCH = 32
DEC_SEQ = 2048
PAST_LEN = 128

GRID_W = 64
HEAD_DIM = 64
N_Q_HEADS = 8
N_KV_HEADS = 2
GQA_GROUP = N_Q_HEADS // N_KV_HEADS
Q_W = N_Q_HEADS * HEAD_DIM
KV_W = N_KV_HEADS * HEAD_DIM
AXIS_DIM = HEAD_DIM // 2
ROPE_THETA = 10000.0
Q_BLOCK = 128
CONV_W = D_MODEL // 2
CONV_K = 31
CONV_PAD = CONV_K // 2
MIX_W = Q_W + CONV_W
IN_W = Q_W + 2 * KV_W + 2 * CONV_W
D_FF = 4 * D_MODEL
EPS = 1e-6
LN_EPS = 1e-5

kernel_name = "hymba_gqa_axialrope_conformer_sqrelu_encoder"


def _rmsnorm(x, g):
    xf = x.astype(jnp.float32)
    y = xf * lax.rsqrt(jnp.mean(xf * xf, axis=-1, keepdims=True) + EPS)
    return (y * g.astype(jnp.float32)).astype(x.dtype)


def _layernorm(x, g, b):
    xf = x.astype(jnp.float32)
    mu = jnp.mean(xf, axis=-1, keepdims=True)
    xc = xf - mu
    var = jnp.mean(xc * xc, axis=-1, keepdims=True)
    y = xc * lax.rsqrt(var + LN_EPS) * g.astype(jnp.float32) + b.astype(jnp.float32)
    return y.astype(x.dtype)


def _axial_rope_tables(seq_len):
    rows = seq_len // GRID_W
    row = jnp.repeat(jnp.arange(rows, dtype=jnp.float32), GRID_W)
    col = jnp.tile(jnp.arange(GRID_W, dtype=jnp.float32), rows)
    inv_freq = ROPE_THETA ** (-jnp.arange(0, AXIS_DIM, 2, dtype=jnp.float32) / AXIS_DIM)
    ang_r = row[:, None] * inv_freq[None, :]
    ang_c = col[:, None] * inv_freq[None, :]
    return jnp.cos(ang_r), jnp.sin(ang_r), jnp.cos(ang_c), jnp.sin(ang_c)


def _rotate(x, cos, sin):
    half = x.shape[-1] // 2
    x1, x2 = x[..., :half], x[..., half:]
    c = cos[None, :, None, :]
    s = sin[None, :, None, :]
    return jnp.concatenate([x1 * c - x2 * s, x2 * c + x1 * s], axis=-1)


def _axial_rope(x, tables):
    cr, sr, cc, sc = tables
    xf = x.astype(jnp.float32)
    out = jnp.concatenate([_rotate(xf[..., :AXIS_DIM], cr, sr),
                           _rotate(xf[..., AXIS_DIM:], cc, sc)], axis=-1)
    return out.astype(x.dtype)


def _attention(q, k, v):
    B, S = q.shape[0], q.shape[1]
    nblk = S // Q_BLOCK
    qb = q.reshape(B, nblk, Q_BLOCK, N_KV_HEADS, GQA_GROUP, HEAD_DIM).transpose(1, 0, 2, 3, 4, 5)
    scale = 1.0 / math.sqrt(HEAD_DIM)

    def one_block(qblk):
        s = jnp.einsum('bqkgd,bskd->bkgqs', qblk, k,
                       preferred_element_type=jnp.float32) * scale
        p = jax.nn.softmax(s, axis=-1).astype(v.dtype)
        return jnp.einsum('bkgqs,bskd->bqkgd', p, v)

    o = lax.map(one_block, qb)
    return o.transpose(1, 0, 2, 3, 4, 5).reshape(B, S, Q_W)


def _conformer_conv(cv, cg, dw_w, dw_b, ln_g, ln_b):
    c = cv * jax.nn.sigmoid(cg)
    c = lax.conv_general_dilated(
        c, dw_w[:, None, :].astype(c.dtype), window_strides=(1,),
        padding=[(CONV_PAD, CONV_PAD)],
        dimension_numbers=('NWC', 'WIO', 'NWC'),
        feature_group_count=CONV_W) + dw_b
    return jax.nn.silu(_layernorm(c, ln_g, ln_b))


def _mixer(h, w_in, q_g, k_g, dw_w, dw_b, ln_g, ln_b, w_out, tables):
    B, S, _ = h.shape
    z = h @ w_in
    q, k, v, cv, cg = jnp.split(
        z, [Q_W, Q_W + KV_W, Q_W + 2 * KV_W, Q_W + 2 * KV_W + CONV_W], axis=-1)
    q = _axial_rope(_rmsnorm(q.reshape(B, S, N_Q_HEADS, HEAD_DIM), q_g), tables)
    k = _axial_rope(_rmsnorm(k.reshape(B, S, N_KV_HEADS, HEAD_DIM), k_g), tables)
    v = v.reshape(B, S, N_KV_HEADS, HEAD_DIM)
    a = _attention(q, k, v)
    c = _conformer_conv(cv, cg, dw_w, dw_b, ln_g, ln_b)
    return jnp.concatenate([a, c], axis=-1) @ w_out


def _mlp(h, w_up, w_down):
    u = h @ w_up
    return jnp.square(jax.nn.relu(u)) @ w_down


def _trunk(x, norm_mix_g, w_in, q_norm_g, k_norm_g, conv_dw_w, conv_dw_b,
           conv_ln_g, conv_ln_b, w_out, norm_mlp_g, w_up, w_down, norm_final_g):
    tables = _axial_rope_tables(x.shape[1])
    h = x
    for l in range(DEPTH):
        h = h + _mixer(_rmsnorm(h, norm_mix_g[l]), w_in[l], q_norm_g[l], k_norm_g[l],
                       conv_dw_w[l], conv_dw_b[l], conv_ln_g[l], conv_ln_b[l],
                       w_out[l], tables)
        h = h + _mlp(_rmsnorm(h, norm_mlp_g[l]), w_up[l], w_down[l])
    return _rmsnorm(h, norm_final_g)


def setup_inputs(seed: int = 0) -> dict:
    key = jax.random.key(seed)
    ks = jax.random.split(key, 16)
    f32 = jnp.float32

    def nrm(k, shape, scale):
        return jax.random.normal(k, shape, f32) * scale

    return {
        "x_prompt": nrm(ks[0], (BATCH, SEQ, D_MODEL), 1.0),
        "x_sample": nrm(ks[1], (DEC_BATCH, DEC_SEQ, D_MODEL), 1.0),
        "norm_mix_g": 1.0 + nrm(ks[2], (DEPTH, D_MODEL), 0.02),
        "w_in": nrm(ks[3], (DEPTH, D_MODEL, IN_W), D_MODEL ** -0.5),
        "q_norm_g": 1.0 + nrm(ks[4], (DEPTH, HEAD_DIM), 0.02),
        "k_norm_g": 1.0 + nrm(ks[5], (DEPTH, HEAD_DIM), 0.02),
        "conv_dw_w": nrm(ks[6], (DEPTH, CONV_K, CONV_W), CONV_K ** -0.5),
        "conv_dw_b": nrm(ks[7], (DEPTH, CONV_W), 0.02),
        "conv_ln_g": 1.0 + nrm(ks[8], (DEPTH, CONV_W), 0.02),
        "conv_ln_b": nrm(ks[9], (DEPTH, CONV_W), 0.02),
        "w_out": nrm(ks[10], (DEPTH, MIX_W, D_MODEL), MIX_W ** -0.5),
        "norm_mlp_g": 1.0 + nrm(ks[11], (DEPTH, D_MODEL), 0.02),
        "w_up": nrm(ks[12], (DEPTH, D_MODEL, D_FF), D_MODEL ** -0.5),
        "w_down": nrm(ks[13], (DEPTH, D_FF, D_MODEL), D_FF ** -0.5),
        "norm_final_g": 1.0 + nrm(ks[14], (D_MODEL,), 0.02),
    }


def reference(x_prompt, x_sample, norm_mix_g, w_in, q_norm_g, k_norm_g, conv_dw_w,
              conv_dw_b, conv_ln_g, conv_ln_b, w_out, norm_mlp_g, w_up, w_down,
              norm_final_g):
    y_prompt = _trunk(x_prompt, norm_mix_g, w_in, q_norm_g, k_norm_g, conv_dw_w,
                      conv_dw_b, conv_ln_g, conv_ln_b, w_out, norm_mlp_g, w_up,
                      w_down, norm_final_g)
    y_sample = _trunk(x_sample, norm_mix_g, w_in, q_norm_g, k_norm_g, conv_dw_w,
                      conv_dw_b, conv_ln_g, conv_ln_b, w_out, norm_mlp_g, w_up,
                      w_down, norm_final_g)
    return (y_prompt, y_sample)
```

```python
import functools
import math

import jax
import jax.numpy as jnp
from jax import lax
from jax.experimental import pallas as pl
from jax.experimental.pallas import tpu as pltpu

F32 = jnp.float32
BF16 = jnp.bfloat16

D_MODEL = 1024
GRID_W = 64
HEAD_DIM = 64
N_Q_HEADS = 8
N_KV_HEADS = 2
GQA_GROUP = N_Q_HEADS // N_KV_HEADS
Q_W = N_Q_HEADS * HEAD_DIM
KV_W = N_KV_HEADS * HEAD_DIM
QKV_W = Q_W + 2 * KV_W
AXIS_DIM = HEAD_DIM // 2
ROPE_HALF = AXIS_DIM // 2
ROPE_THETA = 10000.0
CONV_W = D_MODEL // 2
CONV_K = 31
CONV_PAD = CONV_K // 2
D_FF = 4 * D_MODEL
EPS = 1e-6
LN_EPS = 1e-5

V7X_VMEM_BYTES = 64 * 1024 * 1024
LANES = 128
SUBLANES = 8
LANE_TILES = CONV_W // LANES

TOKEN_TILE = 512
HALO = 2 * SUBLANES
CONV_ROWS = 128
FF_CHUNK = 1024
NEG_BIG = -1e30

assert HALO >= CONV_PAD and TOKEN_TILE % HALO == 0 and TOKEN_TILE % CONV_ROWS == 0


def _vmem_limit(nbytes):
    return int(min(nbytes, V7X_VMEM_BYTES - 6 * 1024 * 1024))


def _proj_kernel(x_ref, g_ref, wqkv_t_ref, wc_ref, tab_ref, q_t_ref, k_ref, v_t_ref, c_ref):
    x = x_ref[0]
    ms = jnp.mean(x * x, axis=-1, keepdims=True)
    xn = (x * lax.rsqrt(ms + EPS) * g_ref[...]).astype(BF16)

    zc = jnp.dot(xn, wc_ref[...], preferred_element_type=F32)
    cv = zc[:, :CONV_W]
    cg = zc[:, CONV_W:]
    c_ref[0] = cv * (1.0 / (1.0 + jnp.exp(-cg)))

    z_t = lax.dot_general(wqkv_t_ref[...], xn, (((1,), (1,)), ((), ())),
                          preferred_element_type=F32)

    def norm_rope(z, a, b):
        r = lax.rsqrt(jnp.mean(z * z, axis=0, keepdims=True) + EPS)
        h = ROPE_HALF
        rot = jnp.concatenate([z[h:2 * h], z[0:h], z[3 * h:4 * h], z[2 * h:3 * h]], axis=0)
        return (z * a + rot * b) * r

    for h in range(N_Q_HEADS):
        zq = z_t[h * HEAD_DIM:(h + 1) * HEAD_DIM]
        q_t_ref[0, h, 0] = norm_rope(zq, tab_ref[0], tab_ref[1]).astype(BF16)
    ks = []
    for h in range(N_KV_HEADS):
        zk = z_t[Q_W + h * HEAD_DIM:Q_W + (h + 1) * HEAD_DIM]
        ks.append(norm_rope(zk, tab_ref[2], tab_ref[3]))
        zv = z_t[Q_W + KV_W + h * HEAD_DIM:Q_W + KV_W + (h + 1) * HEAD_DIM]
        v_t_ref[0, h, 0] = zv.astype(BF16)
    k_nat = jnp.concatenate(ks, axis=0).T
    for h in range(N_KV_HEADS):
        k_ref[0, h] = k_nat[:, h * HEAD_DIM:(h + 1) * HEAD_DIM].astype(BF16)


def _project(x, g, wqkv_t, wc, tables):
    b, s, _ = x.shape
    t = TOKEN_TILE
    n = s // t
    const = lambda *_: (0, 0)
    return pl.pallas_call(
        _proj_kernel,
        grid=(b, n),
        in_specs=[
            pl.BlockSpec((1, t, D_MODEL), lambda i, j: (i, j, 0)),
            pl.BlockSpec((1, D_MODEL), const),
            pl.BlockSpec((QKV_W, D_MODEL), const),
            pl.BlockSpec((D_MODEL, 2 * CONV_W), const),
            pl.BlockSpec((4, HEAD_DIM, t), lambda i, j: (0, 0, j)),
        ],
        out_specs=[
            pl.BlockSpec((1, N_Q_HEADS, 1, HEAD_DIM, t), lambda i, j: (i, 0, j, 0, 0)),
            pl.BlockSpec((1, N_KV_HEADS, t, HEAD_DIM), lambda i, j: (i, 0, j, 0)),
            pl.BlockSpec((1, N_KV_HEADS, 1, HEAD_DIM, t), lambda i, j: (i, 0, j, 0, 0)),
            pl.BlockSpec((1, t, CONV_W), lambda i, j: (i, j, 0)),
        ],
        out_shape=[
            jax.ShapeDtypeStruct((b, N_Q_HEADS, n, HEAD_DIM, t), BF16),
            jax.ShapeDtypeStruct((b, N_KV_HEADS, s, HEAD_DIM), BF16),
            jax.ShapeDtypeStruct((b, N_KV_HEADS, n, HEAD_DIM, t), BF16),
            jax.ShapeDtypeStruct((b, s, CONV_W), F32),
        ],
        compiler_params=pltpu.CompilerParams(
            dimension_semantics=("arbitrary", "arbitrary"),
            vmem_limit_bytes=_vmem_limit(48 * 1024 * 1024)),
        name="proj",
    )(x, g, wqkv_t, wc, tables)


def _attn_kernel(q_t_ref, k_ref, v_t_ref, o_ref, *, n_chunks):
    t = TOKEN_TILE
    outs = []
    for h in range(GQA_GROUP):
        q_t = q_t_ref[0, h, 0]

        def body(c, carry, q_t=q_t):
            m, l, acc = carry
            k = k_ref[0, 0, pl.ds(pl.multiple_of(c * t, t), t), :]
            s = jnp.dot(k, q_t, preferred_element_type=F32)
            m_new = jnp.maximum(m, jnp.max(s, axis=0, keepdims=True))
            alpha = jnp.exp2(m - m_new)
            p = jnp.exp2(s - m_new)
            l = alpha * l + jnp.sum(p, axis=0, keepdims=True)
            pv = jnp.dot(v_t_ref[0, 0, c], p.astype(BF16), preferred_element_type=F32)
            return m_new, l, alpha * acc + pv

        init = (jnp.full((1, t), NEG_BIG, F32), jnp.zeros((1, t), F32),
                jnp.zeros((HEAD_DIM, t), F32))
        _, l, acc = lax.fori_loop(0, n_chunks, body, init)
        outs.append(acc / l)
    o_ref[0] = jnp.concatenate(outs, axis=0).T.astype(BF16)


def _attention(q_t, k, v_t):
    b, _, n, _, t = q_t.shape
    s = n * t
    gw = GQA_GROUP * HEAD_DIM
    return pl.pallas_call(
        functools.partial(_attn_kernel, n_chunks=n),
        grid=(b, N_KV_HEADS, n),
        in_specs=[
            pl.BlockSpec((1, GQA_GROUP, 1, HEAD_DIM, t), lambda i, g, j: (i, g, j, 0, 0)),
            pl.BlockSpec((1, 1, s, HEAD_DIM), lambda i, g, j: (i, g, 0, 0)),
            pl.BlockSpec((1, 1, n, HEAD_DIM, t), lambda i, g, j: (i, g, 0, 0, 0)),
        ],
        out_specs=pl.BlockSpec((1, t, gw), lambda i, g, j: (i, j, g)),
        out_shape=jax.ShapeDtypeStruct((b, s, Q_W), BF16),
        compiler_params=pltpu.CompilerParams(
            dimension_semantics=("arbitrary", "arbitrary", "arbitrary"),
            vmem_limit_bytes=_vmem_limit(48 * 1024 * 1024)),
        name="attn",
    )(q_t, k, v_t)


def _tail_kernel(x_ref, a_ref, c_ref, cp_ref, cn_ref, dww_ref, dwb_ref, lng_ref, lnb_ref,
                 wout_ref, gm_ref, wup_ref, wdn_ref, gf_ref, y_ref, cext_ref, conv_ref):
    t = TOKEN_TILE
    j = pl.program_id(1)
    has_prev = (j > 0).astype(F32)
    has_next = (j < pl.num_programs(1) - 1).astype(F32)
    hr = HALO * LANE_TILES
    cext_ref[0:hr] = cp_ref[0] * has_prev
    cext_ref[hr:hr + t * LANE_TILES] = c_ref[0]
    cext_ref[hr + t * LANE_TILES:] = cn_ref[0] * has_next

    nblk = CONV_ROWS // SUBLANES
    for q in range(LANE_TILES):
        lanes = slice(q * LANES, (q + 1) * LANES)
        taps = [jnp.broadcast_to(dww_ref[tap:tap + 1, lanes], (SUBLANES, LANES))
                for tap in range(CONV_K)]
        bias = jnp.broadcast_to(dwb_ref[:, lanes], (SUBLANES, LANES))

        def conv_rows(r, carry, q=q, lanes=lanes, taps=taps, bias=bias):
            base = pl.multiple_of(r * CONV_ROWS, CONV_ROWS)
            acc = [bias] * nblk
            for w in range(CONV_ROWS - SUBLANES + CONV_K):
                row = (base + w + HALO - CONV_PAD) * LANE_TILES + q
                x = cext_ref[pl.ds(row, SUBLANES, stride=LANE_TILES), :]
                for blk in range(nblk):
                    tap = w - blk * SUBLANES
                    if 0 <= tap < CONV_K:
                        acc[blk] = acc[blk] + taps[tap] * x
            conv_ref[pl.ds(base, CONV_ROWS), lanes] = jnp.concatenate(acc, axis=0)
            return carry

        lax.fori_loop(0, t // CONV_ROWS, conv_rows, 0)

    cc = conv_ref[...]
    mu = jnp.mean(cc, axis=-1, keepdims=True)
    xc = cc - mu
    var = jnp.mean(xc * xc, axis=-1, keepdims=True)
    ln = xc * lax.rsqrt(var + LN_EPS) * lng_ref[...] + lnb_ref[...]
    act = ln * (1.0 / (1.0 + jnp.exp(-ln)))

    mix = jnp.dot(a_ref[0], wout_ref[:Q_W, :], preferred_element_type=F32)
    mix = mix + jnp.dot(act.astype(BF16), wout_ref[Q_W:, :], preferred_element_type=F32)
    h = x_ref[0] + mix

    hn = h * lax.rsqrt(jnp.mean(h * h, axis=-1, keepdims=True) + EPS) * gm_ref[...]
    hn = hn.astype(BF16)
    mlp = jnp.zeros((t, D_MODEL), F32)
    for f in range(D_FF // FF_CHUNK):
        u = jnp.dot(hn, wup_ref[:, f * FF_CHUNK:(f + 1) * FF_CHUNK],
                    preferred_element_type=F32)
        u = jnp.maximum(u, 0.0)
        mlp = mlp + jnp.dot((u * u).astype(BF16), wdn_ref[f * FF_CHUNK:(f + 1) * FF_CHUNK, :],
                            preferred_element_type=F32)
    h2 = h + mlp
    y_ref[0] = h2 * lax.rsqrt(jnp.mean(h2 * h2, axis=-1, keepdims=True) + EPS) * gf_ref[...]


def _tail(x, attn, c, dww, dwb, lng, lnb, wout, gm, wup, wdn, gf):
    b, s, _ = x.shape
    t = TOKEN_TILE
    n = s // t
    hb = t // HALO
    last = s // HALO - 1
    c4 = c.reshape(b, s * LANE_TILES, LANES)
    const = lambda *_: (0, 0)
    resident = dict(pipeline_mode=pl.Buffered(1))
    return pl.pallas_call(
        _tail_kernel,
        grid=(b, n),
        in_specs=[
            pl.BlockSpec((1, t, D_MODEL), lambda i, j: (i, j, 0)),
            pl.BlockSpec((1, t, Q_W), lambda i, j: (i, j, 0)),
            pl.BlockSpec((1, t * LANE_TILES, LANES), lambda i, j: (i, j, 0)),
            pl.BlockSpec((1, HALO * LANE_TILES, LANES),
                         lambda i, j: (i, jnp.maximum(j * hb - 1, 0), 0)),
            pl.BlockSpec((1, HALO * LANE_TILES, LANES),
                         lambda i, j: (i, jnp.minimum((j + 1) * hb, last), 0)),
            pl.BlockSpec((CONV_K, CONV_W), const),
            pl.BlockSpec((1, CONV_W), const),
            pl.BlockSpec((1, CONV_W), const),
            pl.BlockSpec((1, CONV_W), const),
            pl.BlockSpec((Q_W + CONV_W, D_MODEL), const, **resident),
            pl.BlockSpec((1, D_MODEL), const),
            pl.BlockSpec((D_MODEL, D_FF), const, **resident),
            pl.BlockSpec((D_FF, D_MODEL), const, **resident),
            pl.BlockSpec((1, D_MODEL), const),
        ],
        out_specs=pl.BlockSpec((1, t, D_MODEL), lambda i, j: (i, j, 0)),
        out_shape=jax.ShapeDtypeStruct((b, s, D_MODEL), F32),
        scratch_shapes=[
            pltpu.VMEM(((t + 2 * HALO) * LANE_TILES, LANES), F32),
            pltpu.VMEM((t, CONV_W), F32),
        ],
        compiler_params=pltpu.CompilerParams(
            dimension_semantics=("arbitrary", "arbitrary"),
            vmem_limit_bytes=_vmem_limit(V7X_VMEM_BYTES)),
        name="tail",
    )(x, attn, c4, c4, c4, dww, dwb, lng, lnb, wout, gm, wup, wdn, gf)


def _rope_tables(seq_len, q_gain, k_gain):
    pos = jnp.arange(seq_len, dtype=jnp.int32)
    row = (pos // GRID_W).astype(F32)
    col = (pos % GRID_W).astype(F32)
    inv_freq = ROPE_THETA ** (-jnp.arange(0, AXIS_DIM, 2, dtype=F32) / AXIS_DIM)
    ang_r = row[:, None] * inv_freq[None, :]
    ang_c = col[:, None] * inv_freq[None, :]
    cr, sr, cc, sc = jnp.cos(ang_r), jnp.sin(ang_r), jnp.cos(ang_c), jnp.sin(ang_c)
    cos = jnp.concatenate([cr, cr, cc, cc], axis=-1)
    sin = jnp.concatenate([-sr, sr, -sc, sc], axis=-1)

    def rot(g):
        h = ROPE_HALF
        return jnp.concatenate([g[h:2 * h], g[0:h], g[3 * h:4 * h], g[2 * h:3 * h]])

    q_scale = math.log2(math.e) / math.sqrt(HEAD_DIM)
    tabs = [cos * q_gain * q_scale, sin * rot(q_gain) * q_scale,
            cos * k_gain, sin * rot(k_gain)]
    return jnp.stack([tb.T for tb in tabs], axis=0)


def _trunk(x, tables, p):
    q_t, k, v_t, c = _project(x, p["g_mix"], p["wqkv_t"], p["wc"], tables[:, :, :x.shape[1]])
    attn = _attention(q_t, k, v_t)
    return _tail(x, attn, c, p["dww"], p["dwb"], p["lng"], p["lnb"], p["wout"], p["g_mlp"],
                 p["wup"], p["wdn"], p["g_fin"])


def kernel(x_prompt, x_sample, norm_mix_g, w_in, q_norm_g, k_norm_g, conv_dw_w, conv_dw_b,
           conv_ln_g, conv_ln_b, w_out, norm_mlp_g, w_up, w_down, norm_final_g):
    assert w_in.shape[0] == 1, "single-layer trunk"
    w = w_in[0]
    params = dict(
        g_mix=norm_mix_g[0][None, :],
        wqkv_t=w[:, :QKV_W].T.astype(BF16),
        wc=w[:, QKV_W:].astype(BF16),
        dww=conv_dw_w[0], dwb=conv_dw_b[0][None, :],
        lng=conv_ln_g[0][None, :], lnb=conv_ln_b[0][None, :],
        wout=w_out[0].astype(BF16), g_mlp=norm_mlp_g[0][None, :],
        wup=w_up[0].astype(BF16), wdn=w_down[0].astype(BF16),
        g_fin=norm_final_g[None, :],
    )
    max_s = max(x_prompt.shape[1], x_sample.shape[1])
    tables = _rope_tables(max_s, q_norm_g[0], k_norm_g[0])
    return (_trunk(x_prompt, tables, params), _trunk(x_sample, tables, params))
```

```python
import functools
import math

import jax
import jax.numpy as jnp
from jax import lax
from jax.experimental import pallas as pl
from jax.experimental.pallas import tpu as pltpu

F32 = jnp.float32
BF16 = jnp.bfloat16

D_MODEL = 1024
GRID_W = 64
HEAD_DIM = 64
N_Q_HEADS = 8
N_KV_HEADS = 2
GQA_GROUP = N_Q_HEADS // N_KV_HEADS
Q_W = N_Q_HEADS * HEAD_DIM
KV_W = N_KV_HEADS * HEAD_DIM
QKV_W = Q_W + 2 * KV_W
AXIS_DIM = HEAD_DIM // 2
ROPE_HALF = AXIS_DIM // 2
ROPE_THETA = 10000.0
CONV_W = D_MODEL // 2
CONV_K = 31
CONV_PAD = CONV_K // 2
D_FF = 4 * D_MODEL
EPS = 1e-6
LN_EPS = 1e-5

V7X_VMEM_BYTES = 64 * 1024 * 1024
LANES = 128
SUBLANES = 8
LANE_TILES = CONV_W // LANES

TOKEN_TILE = 512
HALO = 2 * SUBLANES
CONV_ROWS = 128
FF_CHUNK = 1024
ATTN_UNROLL = 4
ONES_ROWS = 16
NEG_BIG = -1e30

assert HALO >= CONV_PAD and TOKEN_TILE % HALO == 0 and TOKEN_TILE % CONV_ROWS == 0


def _vmem_limit(nbytes):
    return int(min(nbytes, V7X_VMEM_BYTES - 6 * 1024 * 1024))


def _proj_kernel(x_ref, g_ref, wqkv_t_ref, wc_ref, tab_ref, q_t_ref, k_ref, v_t_ref, c_ref):
    x = x_ref[0]
    ms = jnp.mean(x * x, axis=-1, keepdims=True)
    xn = (x * lax.rsqrt(ms + EPS) * g_ref[...]).astype(BF16)

    zc = jnp.dot(xn, wc_ref[...], preferred_element_type=F32)
    cv = zc[:, :CONV_W]
    cg = zc[:, CONV_W:]
    c_ref[0] = cv * (1.0 / (1.0 + jnp.exp(-cg)))

    z_t = lax.dot_general(wqkv_t_ref[...], xn, (((1,), (1,)), ((), ())),
                          preferred_element_type=F32)

    def norm_rope(z, a, b):
        r = lax.rsqrt(jnp.mean(z * z, axis=0, keepdims=True) + EPS)
        h = ROPE_HALF
        rot = jnp.concatenate([z[h:2 * h], z[0:h], z[3 * h:4 * h], z[2 * h:3 * h]], axis=0)
        return (z * a + rot * b) * r

    for h in range(N_Q_HEADS):
        zq = z_t[h * HEAD_DIM:(h + 1) * HEAD_DIM]
        q_t_ref[0, h, 0] = norm_rope(zq, tab_ref[0], tab_ref[1]).astype(BF16)
    ks = []
    for h in range(N_KV_HEADS):
        zk = z_t[Q_W + h * HEAD_DIM:Q_W + (h + 1) * HEAD_DIM]
        ks.append(norm_rope(zk, tab_ref[2], tab_ref[3]))
        zv = z_t[Q_W + KV_W + h * HEAD_DIM:Q_W + KV_W + (h + 1) * HEAD_DIM]
        v_t_ref[0, h, 0] = zv.astype(BF16)
    k_nat = jnp.concatenate(ks, axis=0).T
    for h in range(N_KV_HEADS):
        k_ref[0, h] = k_nat[:, h * HEAD_DIM:(h + 1) * HEAD_DIM].astype(BF16)


def _project(x, g, wqkv_t, wc, tables):
    b, s, _ = x.shape
    t = TOKEN_TILE
    n = s // t
    const = lambda *_: (0, 0)
    return pl.pallas_call(
        _proj_kernel,
        grid=(b, n),
        in_specs=[
            pl.BlockSpec((1, t, D_MODEL), lambda i, j: (i, j, 0)),
            pl.BlockSpec((1, D_MODEL), const),
            pl.BlockSpec((QKV_W, D_MODEL), const),
            pl.BlockSpec((D_MODEL, 2 * CONV_W), const),
            pl.BlockSpec((4, HEAD_DIM, t), lambda i, j: (0, 0, j)),
        ],
        out_specs=[
            pl.BlockSpec((1, N_Q_HEADS, 1, HEAD_DIM, t), lambda i, j: (i, 0, j, 0, 0)),
            pl.BlockSpec((1, N_KV_HEADS, t, HEAD_DIM), lambda i, j: (i, 0, j, 0)),
            pl.BlockSpec((1, N_KV_HEADS, 1, HEAD_DIM, t), lambda i, j: (i, 0, j, 0, 0)),
            pl.BlockSpec((1, t, CONV_W), lambda i, j: (i, j, 0)),
        ],
        out_shape=[
            jax.ShapeDtypeStruct((b, N_Q_HEADS, n, HEAD_DIM, t), BF16),
            jax.ShapeDtypeStruct((b, N_KV_HEADS, s, HEAD_DIM), BF16),
            jax.ShapeDtypeStruct((b, N_KV_HEADS, n, HEAD_DIM, t), BF16),
            jax.ShapeDtypeStruct((b, s, CONV_W), F32),
        ],
        compiler_params=pltpu.CompilerParams(
            dimension_semantics=("arbitrary", "arbitrary"),
            vmem_limit_bytes=_vmem_limit(48 * 1024 * 1024)),
        name="proj",
    )(x, g, wqkv_t, wc, tables)


def _attn_kernel(q_t_ref, k_ref, v_t_ref, o_ref, s_ref, p_ref, m_ref, a_ref, acc_ref, *,
                 n_chunks):
    t = TOKEN_TILE
    g = GQA_GROUP
    ones = jnp.ones((ONES_ROWS, t), BF16)
    m_ref[...] = jnp.full(m_ref.shape, NEG_BIG, F32)
    acc_ref[...] = jnp.zeros(acc_ref.shape, F32)
    p_ref[1] = jnp.zeros((t, t), BF16)
    a_ref[1] = jnp.ones((1, t), F32)

    def scores(c, h):
        k = k_ref[0, 0, pl.ds(pl.multiple_of(c * t, t), t), :]
        s_ref[h] = jnp.dot(k, q_t_ref[0, h, 0], preferred_element_type=F32)

    def softmax(h, slot):
        s = s_ref[h]
        m_old = m_ref[h]
        m_new = jnp.maximum(m_old, jnp.max(s, axis=0, keepdims=True))
        a_ref[slot] = jnp.exp2(m_old - m_new)
        m_ref[h] = m_new
        p_ref[slot] = jnp.exp2(s - m_new).astype(BF16)

    def values(c, h, slot):
        v_aug = jnp.concatenate([v_t_ref[0, 0, c], ones], axis=0)
        acc_ref[h] = a_ref[slot] * acc_ref[h] + jnp.dot(v_aug, p_ref[slot],
                                                        preferred_element_type=F32)

    scores(0, 0)
    scores(0, 1)

    def body(cc, carry):
        for u in range(ATTN_UNROLL):
            c = cc * ATTN_UNROLL + u
            for h in range(g):
                slot = h % 2
                if h + 2 < g:
                    scores(c, h + 2)
                else:
                    scores(jnp.minimum(c + 1, n_chunks - 1), h + 2 - g)
                softmax(h, slot)
                if h > 0:
                    values(c, h - 1, 1 - slot)
                else:
                    values(jnp.maximum(c - 1, 0), g - 1, 1 - slot)
        return carry

    lax.fori_loop(0, n_chunks // ATTN_UNROLL, body, 0)
    values(n_chunks - 1, g - 1, 1)
    outs = [acc_ref[h, :HEAD_DIM] / acc_ref[h, HEAD_DIM:HEAD_DIM + 1] for h in range(g)]
    o_ref[0] = jnp.concatenate(outs, axis=0).T.astype(BF16)


def _attention(q_t, k, v_t):
    b, _, n, _, t = q_t.shape
    s = n * t
    gw = GQA_GROUP * HEAD_DIM
    assert n % ATTN_UNROLL == 0 and GQA_GROUP % 2 == 0
    return pl.pallas_call(
        functools.partial(_attn_kernel, n_chunks=n),
        grid=(b, N_KV_HEADS, n),
        in_specs=[
            pl.BlockSpec((1, GQA_GROUP, 1, HEAD_DIM, t), lambda i, g, j: (i, g, j, 0, 0)),
            pl.BlockSpec((1, 1, s, HEAD_DIM), lambda i, g, j: (i, g, 0, 0)),
            pl.BlockSpec((1, 1, n, HEAD_DIM, t), lambda i, g, j: (i, g, 0, 0, 0)),
        ],
        out_specs=pl.BlockSpec((1, t, gw), lambda i, g, j: (i, j, g)),
        out_shape=jax.ShapeDtypeStruct((b, s, Q_W), BF16),
        scratch_shapes=[
            pltpu.VMEM((GQA_GROUP, t, t), F32),
            pltpu.VMEM((2, t, t), BF16),
            pltpu.VMEM((GQA_GROUP, 1, t), F32),
            pltpu.VMEM((2, 1, t), F32),
            pltpu.VMEM((GQA_GROUP, HEAD_DIM + ONES_ROWS, t), F32),
        ],
        compiler_params=pltpu.CompilerParams(
            dimension_semantics=("arbitrary", "arbitrary", "arbitrary"),
            vmem_limit_bytes=_vmem_limit(48 * 1024 * 1024)),
        name="attn",
    )(q_t, k, v_t)


def _tail_kernel(x_ref, a_ref, c_ref, cp_ref, cn_ref, dww_ref, dwb_ref, lng_ref, lnb_ref,
                 wout_ref, gm_ref, wup_ref, wdn_ref, gf_ref, y_ref, cext_ref, conv_ref):
    t = TOKEN_TILE
    j = pl.program_id(1)
    has_prev = (j > 0).astype(F32)
    has_next = (j < pl.num_programs(1) - 1).astype(F32)
    for q in range(LANE_TILES):
        lanes = slice(q * LANES, (q + 1) * LANES)
        rows = lambda tok, n, q=q: pl.ds(tok * LANE_TILES + q, n, stride=LANE_TILES)
        cext_ref[rows(0, HALO), :] = cp_ref[0, :, lanes] * has_prev
        cext_ref[rows(HALO, t), :] = c_ref[0, :, lanes]
        cext_ref[rows(HALO + t, HALO), :] = cn_ref[0, :, lanes] * has_next

    nblk = CONV_ROWS // SUBLANES
    for q in range(LANE_TILES):
        lanes = slice(q * LANES, (q + 1) * LANES)
        taps = [jnp.broadcast_to(dww_ref[tap:tap + 1, lanes], (SUBLANES, LANES))
                for tap in range(CONV_K)]
        bias = jnp.broadcast_to(dwb_ref[:, lanes], (SUBLANES, LANES))

        def conv_rows(r, carry, q=q, lanes=lanes, taps=taps, bias=bias):
            base = pl.multiple_of(r * CONV_ROWS, CONV_ROWS)
            acc = [bias] * nblk
            for w in range(CONV_ROWS - SUBLANES + CONV_K):
                row = (base + w + HALO - CONV_PAD) * LANE_TILES + q
                x = cext_ref[pl.ds(row, SUBLANES, stride=LANE_TILES), :]
                for blk in range(nblk):
                    tap = w - blk * SUBLANES
                    if 0 <= tap < CONV_K:
                        acc[blk] = acc[blk] + taps[tap] * x
            conv_ref[pl.ds(base, CONV_ROWS), lanes] = jnp.concatenate(acc, axis=0)
            return carry

        lax.fori_loop(0, t // CONV_ROWS, conv_rows, 0)

    cc = conv_ref[...]
    mu = jnp.mean(cc, axis=-1, keepdims=True)
    xc = cc - mu
    var = jnp.mean(xc * xc, axis=-1, keepdims=True)
    ln = xc * lax.rsqrt(var + LN_EPS) * lng_ref[...] + lnb_ref[...]
    act = ln * (1.0 / (1.0 + jnp.exp(-ln)))

    mix = jnp.dot(a_ref[0], wout_ref[:Q_W, :], preferred_element_type=F32)
    mix = mix + jnp.dot(act.astype(BF16), wout_ref[Q_W:, :], preferred_element_type=F32)
    h = x_ref[0] + mix

    hn = h * lax.rsqrt(jnp.mean(h * h, axis=-1, keepdims=True) + EPS) * gm_ref[...]
    hn = hn.astype(BF16)
    mlp = jnp.zeros((t, D_MODEL), F32)
    for f in range(D_FF // FF_CHUNK):
        u = jnp.dot(hn, wup_ref[:, f * FF_CHUNK:(f + 1) * FF_CHUNK],
                    preferred_element_type=F32)
        u = jnp.maximum(u, 0.0)
        mlp = mlp + jnp.dot((u * u).astype(BF16), wdn_ref[f * FF_CHUNK:(f + 1) * FF_CHUNK, :],
                            preferred_element_type=F32)
    h2 = h + mlp
    y_ref[0] = h2 * lax.rsqrt(jnp.mean(h2 * h2, axis=-1, keepdims=True) + EPS) * gf_ref[...]


def _tail(x, attn, c, dww, dwb, lng, lnb, wout, gm, wup, wdn, gf):
    b, s, _ = x.shape
    t = TOKEN_TILE
    n = s // t
    hb = t // HALO
    last = s // HALO - 1
    const = lambda *_: (0, 0)
    resident = dict(pipeline_mode=pl.Buffered(1))
    return pl.pallas_call(
        _tail_kernel,
        grid=(b, n),
        in_specs=[
            pl.BlockSpec((1, t, D_MODEL), lambda i, j: (i, j, 0)),
            pl.BlockSpec((1, t, Q_W), lambda i, j: (i, j, 0)),
            pl.BlockSpec((1, t, CONV_W), lambda i, j: (i, j, 0)),
            pl.BlockSpec((1, HALO, CONV_W), lambda i, j: (i, jnp.maximum(j * hb - 1, 0), 0)),
            pl.BlockSpec((1, HALO, CONV_W), lambda i, j: (i, jnp.minimum((j + 1) * hb, last), 0)),
            pl.BlockSpec((CONV_K, CONV_W), const),
            pl.BlockSpec((1, CONV_W), const),
            pl.BlockSpec((1, CONV_W), const),
            pl.BlockSpec((1, CONV_W), const),
            pl.BlockSpec((Q_W + CONV_W, D_MODEL), const, **resident),
            pl.BlockSpec((1, D_MODEL), const),
            pl.BlockSpec((D_MODEL, D_FF), const, **resident),
            pl.BlockSpec((D_FF, D_MODEL), const, **resident),
            pl.BlockSpec((1, D_MODEL), const),
        ],
        out_specs=pl.BlockSpec((1, t, D_MODEL), lambda i, j: (i, j, 0)),
        out_shape=jax.ShapeDtypeStruct((b, s, D_MODEL), F32),
        scratch_shapes=[
            pltpu.VMEM(((t + 2 * HALO) * LANE_TILES, LANES), F32),
            pltpu.VMEM((t, CONV_W), F32),
        ],
        compiler_params=pltpu.CompilerParams(
            dimension_semantics=("arbitrary", "arbitrary"),
            vmem_limit_bytes=_vmem_limit(V7X_VMEM_BYTES)),
        name="tail",
    )(x, attn, c, c, c, dww, dwb, lng, lnb, wout, gm, wup, wdn, gf)


def _rope_tables(seq_len, q_gain, k_gain):
    pos = jnp.arange(seq_len, dtype=jnp.int32)
    row = (pos // GRID_W).astype(F32)
    col = (pos % GRID_W).astype(F32)
    inv_freq = ROPE_THETA ** (-jnp.arange(0, AXIS_DIM, 2, dtype=F32) / AXIS_DIM)
    ang_r = row[:, None] * inv_freq[None, :]
    ang_c = col[:, None] * inv_freq[None, :]
    cr, sr, cc, sc = jnp.cos(ang_r), jnp.sin(ang_r), jnp.cos(ang_c), jnp.sin(ang_c)
    cos = jnp.concatenate([cr, cr, cc, cc], axis=-1)
    sin = jnp.concatenate([-sr, sr, -sc, sc], axis=-1)

    def rot(g):
        h = ROPE_HALF
        return jnp.concatenate([g[h:2 * h], g[0:h], g[3 * h:4 * h], g[2 * h:3 * h]])

    q_scale = math.log2(math.e) / math.sqrt(HEAD_DIM)
    tabs = [cos * q_gain * q_scale, sin * rot(q_gain) * q_scale,
            cos * k_gain, sin * rot(k_gain)]
    return jnp.stack([tb.T for tb in tabs], axis=0)


def _trunk(x, tables, p):
    q_t, k, v_t, c = _project(x, p["g_mix"], p["wqkv_t"], p["wc"], tables[:, :, :x.shape[1]])
    attn = _attention(q_t, k, v_t)
    return _tail(x, attn, c, p["dww"], p["dwb"], p["lng"], p["lnb"], p["wout"], p["g_mlp"],
                 p["wup"], p["wdn"], p["g_fin"])


def kernel(x_prompt, x_sample, norm_mix_g, w_in, q_norm_g, k_norm_g, conv_dw_w, conv_dw_b,
           conv_ln_g, conv_ln_b, w_out, norm_mlp_g, w_up, w_down, norm_final_g):
    assert w_in.shape[0] == 1, "single-layer trunk"
    w = w_in[0]
    params = dict(
        g_mix=norm_mix_g[0][None, :],
        wqkv_t=w[:, :QKV_W].T.astype(BF16),
        wc=w[:, QKV_W:].astype(BF16),
        dww=conv_dw_w[0], dwb=conv_dw_b[0][None, :],
        lng=conv_ln_g[0][None, :], lnb=conv_ln_b[0][None, :],
        wout=w_out[0].astype(BF16), g_mlp=norm_mlp_g[0][None, :],
        wup=w_up[0].astype(BF16), wdn=w_down[0].astype(BF16),
        g_fin=norm_final_g[None, :],
    )
    max_s = max(x_prompt.shape[1], x_sample.shape[1])
    tables = _rope_tables(max_s, q_norm_g[0], k_norm_g[0])
    return (_trunk(x_prompt, tables, params), _trunk(x_sample, tables, params))
```

```python
import functools
import math

import jax
import jax.numpy as jnp
from jax import lax
from jax.experimental import pallas as pl
from jax.experimental.pallas import tpu as pltpu

F32 = jnp.float32
BF16 = jnp.bfloat16

D_MODEL = 1024
GRID_W = 64
HEAD_DIM = 64
N_Q_HEADS = 8
N_KV_HEADS = 2
GQA_GROUP = N_Q_HEADS // N_KV_HEADS
Q_W = N_Q_HEADS * HEAD_DIM
KV_W = N_KV_HEADS * HEAD_DIM
QKV_W = Q_W + 2 * KV_W
AXIS_DIM = HEAD_DIM // 2
ROPE_HALF = AXIS_DIM // 2
ROPE_THETA = 10000.0
CONV_W = D_MODEL // 2
CONV_K = 31
CONV_PAD = CONV_K // 2
D_FF = 4 * D_MODEL
EPS = 1e-6
LN_EPS = 1e-5

V7X_VMEM_BYTES = 64 * 1024 * 1024
LANES = 128
SUBLANES = 8
LANE_TILES = CONV_W // LANES

TOKEN_TILE = 512
HALO = 2 * SUBLANES
CONV_ROWS = 128
FF_CHUNK = 1024
ATTN_Q_SPLIT = 2
ATTN_AHEAD = 3
ATTN_S_SLOTS = 4
ATTN_MAX_UNROLL = 8
ONES_ROWS = 16
NEG_BIG = -1e30

assert HALO >= CONV_PAD and TOKEN_TILE % HALO == 0 and TOKEN_TILE % CONV_ROWS == 0


def _vmem_limit(nbytes):
    return int(min(nbytes, V7X_VMEM_BYTES - 6 * 1024 * 1024))


def _proj_kernel(x_ref, g_ref, wqkv_t_ref, wc_ref, tab_ref, q_t_ref, k_ref, v_t_ref, c_ref):
    x = x_ref[0]
    ms = jnp.mean(x * x, axis=-1, keepdims=True)
    xn = (x * lax.rsqrt(ms + EPS) * g_ref[...]).astype(BF16)

    zc = jnp.dot(xn, wc_ref[...], preferred_element_type=F32)
    cv = zc[:, :CONV_W]
    cg = zc[:, CONV_W:]
    c_ref[0] = cv * (1.0 / (1.0 + jnp.exp(-cg)))

    z_t = lax.dot_general(wqkv_t_ref[...], xn, (((1,), (1,)), ((), ())),
                          preferred_element_type=F32)

    def norm_rope(z, a, b):
        r = lax.rsqrt(jnp.mean(z * z, axis=0, keepdims=True) + EPS)
        h = ROPE_HALF
        rot = jnp.concatenate([z[h:2 * h], z[0:h], z[3 * h:4 * h], z[2 * h:3 * h]], axis=0)
        return (z * a + rot * b) * r

    for h in range(N_Q_HEADS):
        zq = z_t[h * HEAD_DIM:(h + 1) * HEAD_DIM]
        q_t_ref[0, h, 0] = norm_rope(zq, tab_ref[0], tab_ref[1]).astype(BF16)
    ks = []
    for h in range(N_KV_HEADS):
        zk = z_t[Q_W + h * HEAD_DIM:Q_W + (h + 1) * HEAD_DIM]
        ks.append(norm_rope(zk, tab_ref[2], tab_ref[3]))
        zv = z_t[Q_W + KV_W + h * HEAD_DIM:Q_W + KV_W + (h + 1) * HEAD_DIM]
        v_t_ref[0, h, 0] = zv.astype(BF16)
    k_nat = jnp.concatenate(ks, axis=0).T
    for h in range(N_KV_HEADS):
        k_ref[0, h] = k_nat[:, h * HEAD_DIM:(h + 1) * HEAD_DIM].astype(BF16)


def _project(x, g, wqkv_t, wc, tables):
    b, s, _ = x.shape
    t = TOKEN_TILE
    n = s // t
    const = lambda *_: (0, 0)
    return pl.pallas_call(
        _proj_kernel,
        grid=(b, n),
        in_specs=[
            pl.BlockSpec((1, t, D_MODEL), lambda i, j: (i, j, 0)),
            pl.BlockSpec((1, D_MODEL), const),
            pl.BlockSpec((QKV_W, D_MODEL), const),
            pl.BlockSpec((D_MODEL, 2 * CONV_W), const),
            pl.BlockSpec((4, HEAD_DIM, t), lambda i, j: (0, 0, j)),
        ],
        out_specs=[
            pl.BlockSpec((1, N_Q_HEADS, 1, HEAD_DIM, t), lambda i, j: (i, 0, j, 0, 0)),
            pl.BlockSpec((1, N_KV_HEADS, t, HEAD_DIM), lambda i, j: (i, 0, j, 0)),
            pl.BlockSpec((1, N_KV_HEADS, 1, HEAD_DIM, t), lambda i, j: (i, 0, j, 0, 0)),
            pl.BlockSpec((1, t, CONV_W), lambda i, j: (i, j, 0)),
        ],
        out_shape=[
            jax.ShapeDtypeStruct((b, N_Q_HEADS, n, HEAD_DIM, t), BF16),
            jax.ShapeDtypeStruct((b, N_KV_HEADS, s, HEAD_DIM), BF16),
            jax.ShapeDtypeStruct((b, N_KV_HEADS, n, HEAD_DIM, t), BF16),
            jax.ShapeDtypeStruct((b, s, CONV_W), F32),
        ],
        compiler_params=pltpu.CompilerParams(
            dimension_semantics=("arbitrary", "arbitrary"),
            vmem_limit_bytes=_vmem_limit(48 * 1024 * 1024)),
        name="proj",
    )(x, g, wqkv_t, wc, tables)


def _attn_kernel(q_t_ref, k_ref, v_t_ref, o_ref, s_ref, p_ref, m_ref, a_ref, acc_ref, *,
                 n_chunks, unroll):
    t = TOKEN_TILE
    tq = t // ATTN_Q_SPLIT
    nu = GQA_GROUP * ATTN_Q_SPLIT
    ones = jnp.ones((ONES_ROWS, t), BF16)
    m_ref[...] = jnp.full(m_ref.shape, NEG_BIG, F32)
    acc_ref[...] = jnp.zeros(acc_ref.shape, F32)
    p_ref[(nu - 1) % 2] = jnp.zeros((t, tq), BF16)
    a_ref[(nu - 1) % 2] = jnp.ones((1, tq), F32)

    def scores(c, u):
        h, half = divmod(u, ATTN_Q_SPLIT)
        k = k_ref[0, 0, pl.ds(pl.multiple_of(c * t, t), t), :]
        q_t = q_t_ref[0, h, 0, :, half * tq:(half + 1) * tq]
        s_ref[u % ATTN_S_SLOTS] = jnp.dot(k, q_t, preferred_element_type=F32)

    def softmax(u):
        s = s_ref[u % ATTN_S_SLOTS]
        m_old = m_ref[u]
        m_new = jnp.maximum(m_old, jnp.max(s, axis=0, keepdims=True))
        a_ref[u % 2] = jnp.exp2(m_old - m_new)
        m_ref[u] = m_new
        p_ref[u % 2] = jnp.exp2(s - m_new).astype(BF16)

    def values(c, u):
        v_aug = jnp.concatenate([v_t_ref[0, 0, c], ones], axis=0)
        acc_ref[u] = a_ref[u % 2] * acc_ref[u] + jnp.dot(v_aug, p_ref[u % 2],
                                                         preferred_element_type=F32)

    for u in range(ATTN_AHEAD):
        scores(0, u)

    def body(cc, carry):
        for i in range(unroll):
            c = cc * unroll + i
            for u in range(nu):
                if u + ATTN_AHEAD < nu:
                    scores(c, u + ATTN_AHEAD)
                else:
                    scores(jnp.minimum(c + 1, n_chunks - 1), u + ATTN_AHEAD - nu)
                softmax(u)
                if u > 0:
                    values(c, u - 1)
                else:
                    values(jnp.maximum(c - 1, 0), nu - 1)
        return carry

    lax.fori_loop(0, n_chunks // unroll, body, 0)
    values(n_chunks - 1, nu - 1)
    outs = []
    for h in range(GQA_GROUP):
        halves = [acc_ref[h * ATTN_Q_SPLIT + i, :HEAD_DIM]
                  / acc_ref[h * ATTN_Q_SPLIT + i, HEAD_DIM:HEAD_DIM + 1]
                  for i in range(ATTN_Q_SPLIT)]
        outs.append(jnp.concatenate(halves, axis=1))
    o_ref[0] = jnp.concatenate(outs, axis=0).T.astype(BF16)


def _attention(q_t, k, v_t):
    b, _, n, _, t = q_t.shape
    s = n * t
    gw = GQA_GROUP * HEAD_DIM
    tq = t // ATTN_Q_SPLIT
    nu = GQA_GROUP * ATTN_Q_SPLIT
    unroll = math.gcd(n, ATTN_MAX_UNROLL)
    assert nu % 2 == 0 and nu % ATTN_S_SLOTS == 0 and ATTN_AHEAD < ATTN_S_SLOTS
    return pl.pallas_call(
        functools.partial(_attn_kernel, n_chunks=n, unroll=unroll),
        grid=(b, N_KV_HEADS, n),
        in_specs=[
            pl.BlockSpec((1, GQA_GROUP, 1, HEAD_DIM, t), lambda i, g, j: (i, g, j, 0, 0)),
            pl.BlockSpec((1, 1, s, HEAD_DIM), lambda i, g, j: (i, g, 0, 0)),
            pl.BlockSpec((1, 1, n, HEAD_DIM, t), lambda i, g, j: (i, g, 0, 0, 0)),
        ],
        out_specs=pl.BlockSpec((1, t, gw), lambda i, g, j: (i, j, g)),
        out_shape=jax.ShapeDtypeStruct((b, s, Q_W), BF16),
        scratch_shapes=[
            pltpu.VMEM((ATTN_S_SLOTS, t, tq), F32),
            pltpu.VMEM((2, t, tq), BF16),
            pltpu.VMEM((nu, 1, tq), F32),
            pltpu.VMEM((2, 1, tq), F32),
            pltpu.VMEM((nu, HEAD_DIM + ONES_ROWS, tq), F32),
        ],
        compiler_params=pltpu.CompilerParams(
            dimension_semantics=("arbitrary", "arbitrary", "arbitrary"),
            vmem_limit_bytes=_vmem_limit(48 * 1024 * 1024)),
        name="attn",
    )(q_t, k, v_t)


def _tail_kernel(x_ref, a_ref, c_ref, cp_ref, cn_ref, dww_ref, dwb_ref, lng_ref, lnb_ref,
                 wout_ref, gm_ref, wup_ref, wdn_ref, gf_ref, y_ref, cext_ref, conv_ref):
    t = TOKEN_TILE
    j = pl.program_id(1)
    has_prev = (j > 0).astype(F32)
    has_next = (j < pl.num_programs(1) - 1).astype(F32)
    for q in range(LANE_TILES):
        lanes = slice(q * LANES, (q + 1) * LANES)
        rows = lambda tok, n, q=q: pl.ds(tok * LANE_TILES + q, n, stride=LANE_TILES)
        cext_ref[rows(0, HALO), :] = cp_ref[0, :, lanes] * has_prev
        cext_ref[rows(HALO, t), :] = c_ref[0, :, lanes]
        cext_ref[rows(HALO + t, HALO), :] = cn_ref[0, :, lanes] * has_next

    nblk = CONV_ROWS // SUBLANES
    for q in range(LANE_TILES):
        lanes = slice(q * LANES, (q + 1) * LANES)
        taps = [jnp.broadcast_to(dww_ref[tap:tap + 1, lanes], (SUBLANES, LANES))
                for tap in range(CONV_K)]
        bias = jnp.broadcast_to(dwb_ref[:, lanes], (SUBLANES, LANES))

        def conv_rows(r, carry, q=q, lanes=lanes, taps=taps, bias=bias):
            base = pl.multiple_of(r * CONV_ROWS, CONV_ROWS)
            acc = [bias] * nblk
            for w in range(CONV_ROWS - SUBLANES + CONV_K):
                row = (base + w + HALO - CONV_PAD) * LANE_TILES + q
                x = cext_ref[pl.ds(row, SUBLANES, stride=LANE_TILES), :]
                for blk in range(nblk):
                    tap = w - blk * SUBLANES
                    if 0 <= tap < CONV_K:
                        acc[blk] = acc[blk] + taps[tap] * x
            conv_ref[pl.ds(base, CONV_ROWS), lanes] = jnp.concatenate(acc, axis=0)
            return carry

        lax.fori_loop(0, t // CONV_ROWS, conv_rows, 0)

    cc = conv_ref[...]
    mu = jnp.mean(cc, axis=-1, keepdims=True)
    xc = cc - mu
    var = jnp.mean(xc * xc, axis=-1, keepdims=True)
    ln = xc * lax.rsqrt(var + LN_EPS) * lng_ref[...] + lnb_ref[...]
    act = ln * (1.0 / (1.0 + jnp.exp(-ln)))

    mix = jnp.dot(a_ref[0], wout_ref[:Q_W, :], preferred_element_type=F32)
    mix = mix + jnp.dot(act.astype(BF16), wout_ref[Q_W:, :], preferred_element_type=F32)
    h = x_ref[0] + mix

    hn = h * lax.rsqrt(jnp.mean(h * h, axis=-1, keepdims=True) + EPS) * gm_ref[...]
    hn = hn.astype(BF16)
    mlp = jnp.zeros((t, D_MODEL), F32)
    for f in range(D_FF // FF_CHUNK):
        u = jnp.dot(hn, wup_ref[:, f * FF_CHUNK:(f + 1) * FF_CHUNK],
                    preferred_element_type=F32)
        u = jnp.maximum(u, 0.0)
        mlp = mlp + jnp.dot((u * u).astype(BF16), wdn_ref[f * FF_CHUNK:(f + 1) * FF_CHUNK, :],
                            preferred_element_type=F32)
    h2 = h + mlp
    y_ref[0] = h2 * lax.rsqrt(jnp.mean(h2 * h2, axis=-1, keepdims=True) + EPS) * gf_ref[...]


def _tail(x, attn, c, dww, dwb, lng, lnb, wout, gm, wup, wdn, gf):
    b, s, _ = x.shape
    t = TOKEN_TILE
    n = s // t
    hb = t // HALO
    last = s // HALO - 1
    const = lambda *_: (0, 0)
    resident = dict(pipeline_mode=pl.Buffered(1))
    return pl.pallas_call(
        _tail_kernel,
        grid=(b, n),
        in_specs=[
            pl.BlockSpec((1, t, D_MODEL), lambda i, j: (i, j, 0)),
            pl.BlockSpec((1, t, Q_W), lambda i, j: (i, j, 0)),
            pl.BlockSpec((1, t, CONV_W), lambda i, j: (i, j, 0)),
            pl.BlockSpec((1, HALO, CONV_W), lambda i, j: (i, jnp.maximum(j * hb - 1, 0), 0)),
            pl.BlockSpec((1, HALO, CONV_W), lambda i, j: (i, jnp.minimum((j + 1) * hb, last), 0)),
            pl.BlockSpec((CONV_K, CONV_W), const),
            pl.BlockSpec((1, CONV_W), const),
            pl.BlockSpec((1, CONV_W), const),
            pl.BlockSpec((1, CONV_W), const),
            pl.BlockSpec((Q_W + CONV_W, D_MODEL), const, **resident),
            pl.BlockSpec((1, D_MODEL), const),
            pl.BlockSpec((D_MODEL, D_FF), const, **resident),
            pl.BlockSpec((D_FF, D_MODEL), const, **resident),
            pl.BlockSpec((1, D_MODEL), const),
        ],
        out_specs=pl.BlockSpec((1, t, D_MODEL), lambda i, j: (i, j, 0)),
        out_shape=jax.ShapeDtypeStruct((b, s, D_MODEL), F32),
        scratch_shapes=[
            pltpu.VMEM(((t + 2 * HALO) * LANE_TILES, LANES), F32),
            pltpu.VMEM((t, CONV_W), F32),
        ],
        compiler_params=pltpu.CompilerParams(
            dimension_semantics=("arbitrary", "arbitrary"),
            vmem_limit_bytes=_vmem_limit(V7X_VMEM_BYTES)),
        name="tail",
    )(x, attn, c, c, c, dww, dwb, lng, lnb, wout, gm, wup, wdn, gf)


def _rope_tables(seq_len, q_gain, k_gain):
    pos = jnp.arange(seq_len, dtype=jnp.int32)
    row = (pos // GRID_W).astype(F32)
    col = (pos % GRID_W).astype(F32)
    inv_freq = ROPE_THETA ** (-jnp.arange(0, AXIS_DIM, 2, dtype=F32) / AXIS_DIM)
    ang_r = row[:, None] * inv_freq[None, :]
    ang_c = col[:, None] * inv_freq[None, :]
    cr, sr, cc, sc = jnp.cos(ang_r), jnp.sin(ang_r), jnp.cos(ang_c), jnp.sin(ang_c)
    cos = jnp.concatenate([cr, cr, cc, cc], axis=-1)
    sin = jnp.concatenate([-sr, sr, -sc, sc], axis=-1)

    def rot(g):
        h = ROPE_HALF
        return jnp.concatenate([g[h:2 * h], g[0:h], g[3 * h:4 * h], g[2 * h:3 * h]])

    q_scale = math.log2(math.e) / math.sqrt(HEAD_DIM)
    tabs = [cos * q_gain * q_scale, sin * rot(q_gain) * q_scale,
            cos * k_gain, sin * rot(k_gain)]
    return jnp.stack([tb.T for tb in tabs], axis=0)


def _trunk(x, tables, p):
    q_t, k, v_t, c = _project(x, p["g_mix"], p["wqkv_t"], p["wc"], tables[:, :, :x.shape[1]])
    attn = _attention(q_t, k, v_t)
    return _tail(x, attn, c, p["dww"], p["dwb"], p["lng"], p["lnb"], p["wout"], p["g_mlp"],
                 p["wup"], p["wdn"], p["g_fin"])


def kernel(x_prompt, x_sample, norm_mix_g, w_in, q_norm_g, k_norm_g, conv_dw_w, conv_dw_b,
           conv_ln_g, conv_ln_b, w_out, norm_mlp_g, w_up, w_down, norm_final_g):
    assert w_in.shape[0] == 1, "single-layer trunk"
    w = w_in[0]
    params = dict(
        g_mix=norm_mix_g[0][None, :],
        wqkv_t=w[:, :QKV_W].T.astype(BF16),
        wc=w[:, QKV_W:].astype(BF16),
        dww=conv_dw_w[0], dwb=conv_dw_b[0][None, :],
        lng=conv_ln_g[0][None, :], lnb=conv_ln_b[0][None, :],
        wout=w_out[0].astype(BF16), g_mlp=norm_mlp_g[0][None, :],
        wup=w_up[0].astype(BF16), wdn=w_down[0].astype(BF16),
        g_fin=norm_final_g[None, :],
    )
    max_s = max(x_prompt.shape[1], x_sample.shape[1])
    tables = _rope_tables(max_s, q_norm_g[0], k_norm_g[0])
    return (_trunk(x_prompt, tables, params), _trunk(x_sample, tables, params))
```

```python
import functools
import math

import jax
import jax.numpy as jnp
from jax import lax
from jax.experimental import pallas as pl
from jax.experimental.pallas import tpu as pltpu

F32 = jnp.float32
BF16 = jnp.bfloat16

D_MODEL = 1024
GRID_W = 64
HEAD_DIM = 64
N_Q_HEADS = 8
N_KV_HEADS = 2
GQA_GROUP = N_Q_HEADS // N_KV_HEADS
Q_W = N_Q_HEADS * HEAD_DIM
KV_W = N_KV_HEADS * HEAD_DIM
QKV_W = Q_W + 2 * KV_W
AXIS_DIM = HEAD_DIM // 2
ROPE_HALF = AXIS_DIM // 2
ROPE_THETA = 10000.0
CONV_W = D_MODEL // 2
CONV_K = 31
CONV_PAD = CONV_K // 2
D_FF = 4 * D_MODEL
EPS = 1e-6
LN_EPS = 1e-5

V7X_VMEM_BYTES = 64 * 1024 * 1024
LANES = 128
SUBLANES = 8
LANE_TILES = CONV_W // LANES

TOKEN_TILE = 512
HALO = 2 * SUBLANES
CONV_ROWS = 128
FF_CHUNK = 1024
ATTN_Q_SPLIT = 2
ATTN_AHEAD = 3
ATTN_S_SLOTS = 4
ATTN_MAX_UNROLL = 8
ATTN_BOUNDED_AHEAD = 2
ATTN_P_SLOTS = 4
ATTN_SCORE_BOUND = 40.0
ATTN_BOUND_MARGIN = 1.02
Q_SCALE = math.log2(math.e) / math.sqrt(HEAD_DIM)
ONES_ROWS = 16
NEG_BIG = -1e30

assert HALO >= CONV_PAD and TOKEN_TILE % HALO == 0 and TOKEN_TILE % CONV_ROWS == 0


def _vmem_limit(nbytes):
    return int(min(nbytes, V7X_VMEM_BYTES - 6 * 1024 * 1024))


def _proj_kernel(x_ref, g_ref, wqkv_t_ref, wc_ref, tab_ref, q_t_ref, k_ref, v_t_ref, c_ref):
    x = x_ref[0]
    ms = jnp.mean(x * x, axis=-1, keepdims=True)
    xn = (x * lax.rsqrt(ms + EPS) * g_ref[...]).astype(BF16)

    zc = jnp.dot(xn, wc_ref[...], preferred_element_type=F32)
    cv = zc[:, :CONV_W]
    cg = zc[:, CONV_W:]
    c_ref[0] = cv * (1.0 / (1.0 + jnp.exp(-cg)))

    z_t = lax.dot_general(wqkv_t_ref[...], xn, (((1,), (1,)), ((), ())),
                          preferred_element_type=F32)

    def norm_rope(z, a, b):
        r = lax.rsqrt(jnp.mean(z * z, axis=0, keepdims=True) + EPS)
        h = ROPE_HALF
        rot = jnp.concatenate([z[h:2 * h], z[0:h], z[3 * h:4 * h], z[2 * h:3 * h]], axis=0)
        return (z * a + rot * b) * r

    for h in range(N_Q_HEADS):
        zq = z_t[h * HEAD_DIM:(h + 1) * HEAD_DIM]
        q_t_ref[0, h, 0] = norm_rope(zq, tab_ref[0], tab_ref[1]).astype(BF16)
    ks = []
    for h in range(N_KV_HEADS):
        zk = z_t[Q_W + h * HEAD_DIM:Q_W + (h + 1) * HEAD_DIM]
        ks.append(norm_rope(zk, tab_ref[2], tab_ref[3]))
        zv = z_t[Q_W + KV_W + h * HEAD_DIM:Q_W + KV_W + (h + 1) * HEAD_DIM]
        v_t_ref[0, h, 0] = zv.astype(BF16)
    k_nat = jnp.concatenate(ks, axis=0).T
    for h in range(N_KV_HEADS):
        k_ref[0, h] = k_nat[:, h * HEAD_DIM:(h + 1) * HEAD_DIM].astype(BF16)


def _project(x, g, wqkv_t, wc, tables):
    b, s, _ = x.shape
    t = TOKEN_TILE
    n = s // t
    const = lambda *_: (0, 0)
    return pl.pallas_call(
        _proj_kernel,
        grid=(b, n),
        in_specs=[
            pl.BlockSpec((1, t, D_MODEL), lambda i, j: (i, j, 0)),
            pl.BlockSpec((1, D_MODEL), const),
            pl.BlockSpec((QKV_W, D_MODEL), const),
            pl.BlockSpec((D_MODEL, 2 * CONV_W), const),
            pl.BlockSpec((4, HEAD_DIM, t), lambda i, j: (0, 0, j)),
        ],
        out_specs=[
            pl.BlockSpec((1, N_Q_HEADS, 1, HEAD_DIM, t), lambda i, j: (i, 0, j, 0, 0)),
            pl.BlockSpec((1, N_KV_HEADS, t, HEAD_DIM), lambda i, j: (i, 0, j, 0)),
            pl.BlockSpec((1, N_KV_HEADS, 1, HEAD_DIM, t), lambda i, j: (i, 0, j, 0, 0)),
            pl.BlockSpec((1, t, CONV_W), lambda i, j: (i, j, 0)),
        ],
        out_shape=[
            jax.ShapeDtypeStruct((b, N_Q_HEADS, n, HEAD_DIM, t), BF16),
            jax.ShapeDtypeStruct((b, N_KV_HEADS, s, HEAD_DIM), BF16),
            jax.ShapeDtypeStruct((b, N_KV_HEADS, n, HEAD_DIM, t), BF16),
            jax.ShapeDtypeStruct((b, s, CONV_W), F32),
        ],
        compiler_params=pltpu.CompilerParams(
            dimension_semantics=("arbitrary", "arbitrary"),
            vmem_limit_bytes=_vmem_limit(48 * 1024 * 1024)),
        name="proj",
    )(x, g, wqkv_t, wc, tables)


def _attn_kernel(q_t_ref, k_ref, v_t_ref, o_ref, s_ref, p_ref, m_ref, a_ref, acc_ref, *,
                 n_chunks, unroll):
    t = TOKEN_TILE
    tq = t // ATTN_Q_SPLIT
    nu = GQA_GROUP * ATTN_Q_SPLIT
    ones = jnp.ones((ONES_ROWS, t), BF16)
    m_ref[...] = jnp.full(m_ref.shape, NEG_BIG, F32)
    acc_ref[...] = jnp.zeros(acc_ref.shape, F32)
    p_ref[(nu - 1) % 2] = jnp.zeros((t, tq), BF16)
    a_ref[(nu - 1) % 2] = jnp.ones((1, tq), F32)

    def scores(c, u):
        h, half = divmod(u, ATTN_Q_SPLIT)
        k = k_ref[0, 0, pl.ds(pl.multiple_of(c * t, t), t), :]
        q_t = q_t_ref[0, h, 0, :, half * tq:(half + 1) * tq]
        s_ref[u % ATTN_S_SLOTS] = jnp.dot(k, q_t, preferred_element_type=F32)

    def softmax(u):
        s = s_ref[u % ATTN_S_SLOTS]
        m_old = m_ref[u]
        m_new = jnp.maximum(m_old, jnp.max(s, axis=0, keepdims=True))
        a_ref[u % 2] = jnp.exp2(m_old - m_new)
        m_ref[u] = m_new
        p_ref[u % 2] = jnp.exp2(s - m_new).astype(BF16)

    def values(c, u):
        v_aug = jnp.concatenate([v_t_ref[0, 0, c], ones], axis=0)
        acc_ref[u] = a_ref[u % 2] * acc_ref[u] + jnp.dot(v_aug, p_ref[u % 2],
                                                         preferred_element_type=F32)

    for u in range(ATTN_AHEAD):
        scores(0, u)

    def body(cc, carry):
        for i in range(unroll):
            c = cc * unroll + i
            for u in range(nu):
                if u + ATTN_AHEAD < nu:
                    scores(c, u + ATTN_AHEAD)
                else:
                    scores(jnp.minimum(c + 1, n_chunks - 1), u + ATTN_AHEAD - nu)
                softmax(u)
                if u > 0:
                    values(c, u - 1)
                else:
                    values(jnp.maximum(c - 1, 0), nu - 1)
        return carry

    lax.fori_loop(0, n_chunks // unroll, body, 0)
    values(n_chunks - 1, nu - 1)
    outs = []
    for h in range(GQA_GROUP):
        halves = [acc_ref[h * ATTN_Q_SPLIT + i, :HEAD_DIM]
                  / acc_ref[h * ATTN_Q_SPLIT + i, HEAD_DIM:HEAD_DIM + 1]
                  for i in range(ATTN_Q_SPLIT)]
        outs.append(jnp.concatenate(halves, axis=1))
    o_ref[0] = jnp.concatenate(outs, axis=0).T.astype(BF16)


def _attn_bounded_kernel(q_t_ref, k_ref, v_t_ref, o_ref, p_ref, acc_ref, *, n_chunks, unroll):
    t = TOKEN_TILE
    tq = t // ATTN_Q_SPLIT
    nu = GQA_GROUP * ATTN_Q_SPLIT
    ones = jnp.ones((ONES_ROWS, t), BF16)
    acc_ref[...] = jnp.zeros(acc_ref.shape, F32)
    p_ref[(nu - 1) % ATTN_P_SLOTS] = jnp.zeros((t, tq), BF16)

    def probs(c, u):
        h, half = divmod(u, ATTN_Q_SPLIT)
        k = k_ref[0, 0, pl.ds(pl.multiple_of(c * t, t), t), :]
        q_t = q_t_ref[0, h, 0, :, half * tq:(half + 1) * tq]
        s = jnp.dot(k, q_t, preferred_element_type=F32)
        p_ref[u % ATTN_P_SLOTS] = jnp.exp2(s).astype(BF16)

    def values(c, u):
        v_aug = jnp.concatenate([v_t_ref[0, 0, c], ones], axis=0)
        acc_ref[u] = acc_ref[u] + jnp.dot(v_aug, p_ref[u % ATTN_P_SLOTS],
                                          preferred_element_type=F32)

    for u in range(ATTN_BOUNDED_AHEAD):
        probs(0, u)

    def body(cc, carry):
        for i in range(unroll):
            c = cc * unroll + i
            for u in range(nu):
                if u + ATTN_BOUNDED_AHEAD < nu:
                    probs(c, u + ATTN_BOUNDED_AHEAD)
                else:
                    probs(jnp.minimum(c + 1, n_chunks - 1), u + ATTN_BOUNDED_AHEAD - nu)
                if u > 0:
                    values(c, u - 1)
                else:
                    values(jnp.maximum(c - 1, 0), nu - 1)
        return carry

    lax.fori_loop(0, n_chunks // unroll, body, 0)
    values(n_chunks - 1, nu - 1)
    outs = []
    for h in range(GQA_GROUP):
        halves = [acc_ref[h * ATTN_Q_SPLIT + i, :HEAD_DIM]
                  / acc_ref[h * ATTN_Q_SPLIT + i, HEAD_DIM:HEAD_DIM + 1]
                  for i in range(ATTN_Q_SPLIT)]
        outs.append(jnp.concatenate(halves, axis=1))
    o_ref[0] = jnp.concatenate(outs, axis=0).T.astype(BF16)


def _attention_call(q_t, k, v_t, bounded):
    b, _, n, _, t = q_t.shape
    s = n * t
    gw = GQA_GROUP * HEAD_DIM
    tq = t // ATTN_Q_SPLIT
    nu = GQA_GROUP * ATTN_Q_SPLIT
    unroll = math.gcd(n, ATTN_MAX_UNROLL)
    assert nu % 2 == 0 and nu % ATTN_S_SLOTS == 0 and ATTN_AHEAD < ATTN_S_SLOTS
    assert nu % ATTN_P_SLOTS == 0 and ATTN_BOUNDED_AHEAD + 1 < ATTN_P_SLOTS
    if bounded:
        body = functools.partial(_attn_bounded_kernel, n_chunks=n, unroll=unroll)
        scratch = [
            pltpu.VMEM((ATTN_P_SLOTS, t, tq), BF16),
            pltpu.VMEM((nu, HEAD_DIM + ONES_ROWS, tq), F32),
        ]
    else:
        body = functools.partial(_attn_kernel, n_chunks=n, unroll=unroll)
        scratch = [
            pltpu.VMEM((ATTN_S_SLOTS, t, tq), F32),
            pltpu.VMEM((2, t, tq), BF16),
            pltpu.VMEM((nu, 1, tq), F32),
            pltpu.VMEM((2, 1, tq), F32),
            pltpu.VMEM((nu, HEAD_DIM + ONES_ROWS, tq), F32),
        ]
    return pl.pallas_call(
        body,
        grid=(b, N_KV_HEADS, n),
        in_specs=[
            pl.BlockSpec((1, GQA_GROUP, 1, HEAD_DIM, t), lambda i, g, j: (i, g, j, 0, 0)),
            pl.BlockSpec((1, 1, s, HEAD_DIM), lambda i, g, j: (i, g, 0, 0)),
            pl.BlockSpec((1, 1, n, HEAD_DIM, t), lambda i, g, j: (i, g, 0, 0, 0)),
        ],
        out_specs=pl.BlockSpec((1, t, gw), lambda i, g, j: (i, j, g)),
        out_shape=jax.ShapeDtypeStruct((b, s, Q_W), BF16),
        scratch_shapes=scratch,
        compiler_params=pltpu.CompilerParams(
            dimension_semantics=("arbitrary", "arbitrary", "arbitrary"),
            vmem_limit_bytes=_vmem_limit(48 * 1024 * 1024)),
        name="attn_bounded" if bounded else "attn",
    )(q_t, k, v_t)


def _attention(q_t, k, v_t, scores_bounded):
    return lax.cond(scores_bounded,
                    functools.partial(_attention_call, bounded=True),
                    functools.partial(_attention_call, bounded=False),
                    q_t, k, v_t)


def _tail_kernel(x_ref, a_ref, c_ref, cp_ref, cn_ref, dww_ref, dwb_ref, lng_ref, lnb_ref,
                 wout_ref, gm_ref, wup_ref, wdn_ref, gf_ref, y_ref, cext_ref, conv_ref):
    t = TOKEN_TILE
    j = pl.program_id(1)
    has_prev = (j > 0).astype(F32)
    has_next = (j < pl.num_programs(1) - 1).astype(F32)
    for q in range(LANE_TILES):
        lanes = slice(q * LANES, (q + 1) * LANES)
        rows = lambda tok, n, q=q: pl.ds(tok * LANE_TILES + q, n, stride=LANE_TILES)
        cext_ref[rows(0, HALO), :] = cp_ref[0, :, lanes] * has_prev
        cext_ref[rows(HALO, t), :] = c_ref[0, :, lanes]
        cext_ref[rows(HALO + t, HALO), :] = cn_ref[0, :, lanes] * has_next

    nblk = CONV_ROWS // SUBLANES
    for q in range(LANE_TILES):
        lanes = slice(q * LANES, (q + 1) * LANES)
        taps = [jnp.broadcast_to(dww_ref[tap:tap + 1, lanes], (SUBLANES, LANES))
                for tap in range(CONV_K)]
        bias = jnp.broadcast_to(dwb_ref[:, lanes], (SUBLANES, LANES))

        def conv_rows(r, carry, q=q, lanes=lanes, taps=taps, bias=bias):
            base = pl.multiple_of(r * CONV_ROWS, CONV_ROWS)
            acc = [bias] * nblk
            for w in range(CONV_ROWS - SUBLANES + CONV_K):
                row = (base + w + HALO - CONV_PAD) * LANE_TILES + q
                x = cext_ref[pl.ds(row, SUBLANES, stride=LANE_TILES), :]
                for blk in range(nblk):
                    tap = w - blk * SUBLANES
                    if 0 <= tap < CONV_K:
                        acc[blk] = acc[blk] + taps[tap] * x
            conv_ref[pl.ds(base, CONV_ROWS), lanes] = jnp.concatenate(acc, axis=0)
            return carry

        lax.fori_loop(0, t // CONV_ROWS, conv_rows, 0)

    cc = conv_ref[...]
    mu = jnp.mean(cc, axis=-1, keepdims=True)
    xc = cc - mu
    var = jnp.mean(xc * xc, axis=-1, keepdims=True)
    ln = xc * lax.rsqrt(var + LN_EPS) * lng_ref[...] + lnb_ref[...]
    act = ln * (1.0 / (1.0 + jnp.exp(-ln)))

    mix = jnp.dot(a_ref[0], wout_ref[:Q_W, :], preferred_element_type=F32)
    mix = mix + jnp.dot(act.astype(BF16), wout_ref[Q_W:, :], preferred_element_type=F32)
    h = x_ref[0] + mix

    hn = h * lax.rsqrt(jnp.mean(h * h, axis=-1, keepdims=True) + EPS) * gm_ref[...]
    hn = hn.astype(BF16)
    mlp = jnp.zeros((t, D_MODEL), F32)
    for f in range(D_FF // FF_CHUNK):
        u = jnp.dot(hn, wup_ref[:, f * FF_CHUNK:(f + 1) * FF_CHUNK],
                    preferred_element_type=F32)
        u = jnp.maximum(u, 0.0)
        mlp = mlp + jnp.dot((u * u).astype(BF16), wdn_ref[f * FF_CHUNK:(f + 1) * FF_CHUNK, :],
                            preferred_element_type=F32)
    h2 = h + mlp
    y_ref[0] = h2 * lax.rsqrt(jnp.mean(h2 * h2, axis=-1, keepdims=True) + EPS) * gf_ref[...]


def _tail(x, attn, c, dww, dwb, lng, lnb, wout, gm, wup, wdn, gf):
    b, s, _ = x.shape
    t = TOKEN_TILE
    n = s // t
    hb = t // HALO
    last = s // HALO - 1
    const = lambda *_: (0, 0)
    resident = dict(pipeline_mode=pl.Buffered(1))
    return pl.pallas_call(
        _tail_kernel,
        grid=(b, n),
        in_specs=[
            pl.BlockSpec((1, t, D_MODEL), lambda i, j: (i, j, 0)),
            pl.BlockSpec((1, t, Q_W), lambda i, j: (i, j, 0)),
            pl.BlockSpec((1, t, CONV_W), lambda i, j: (i, j, 0)),
            pl.BlockSpec((1, HALO, CONV_W), lambda i, j: (i, jnp.maximum(j * hb - 1, 0), 0)),
            pl.BlockSpec((1, HALO, CONV_W), lambda i, j: (i, jnp.minimum((j + 1) * hb, last), 0)),
            pl.BlockSpec((CONV_K, CONV_W), const),
            pl.BlockSpec((1, CONV_W), const),
            pl.BlockSpec((1, CONV_W), const),
            pl.BlockSpec((1, CONV_W), const),
            pl.BlockSpec((Q_W + CONV_W, D_MODEL), const, **resident),
            pl.BlockSpec((1, D_MODEL), const),
            pl.BlockSpec((D_MODEL, D_FF), const, **resident),
            pl.BlockSpec((D_FF, D_MODEL), const, **resident),
            pl.BlockSpec((1, D_MODEL), const),
        ],
        out_specs=pl.BlockSpec((1, t, D_MODEL), lambda i, j: (i, j, 0)),
        out_shape=jax.ShapeDtypeStruct((b, s, D_MODEL), F32),
        scratch_shapes=[
            pltpu.VMEM(((t + 2 * HALO) * LANE_TILES, LANES), F32),
            pltpu.VMEM((t, CONV_W), F32),
        ],
        compiler_params=pltpu.CompilerParams(
            dimension_semantics=("arbitrary", "arbitrary"),
            vmem_limit_bytes=_vmem_limit(V7X_VMEM_BYTES)),
        name="tail",
    )(x, attn, c, c, c, dww, dwb, lng, lnb, wout, gm, wup, wdn, gf)


def _rope_tables(seq_len, q_gain, k_gain):
    pos = jnp.arange(seq_len, dtype=jnp.int32)
    row = (pos // GRID_W).astype(F32)
    col = (pos % GRID_W).astype(F32)
    inv_freq = ROPE_THETA ** (-jnp.arange(0, AXIS_DIM, 2, dtype=F32) / AXIS_DIM)
    ang_r = row[:, None] * inv_freq[None, :]
    ang_c = col[:, None] * inv_freq[None, :]
    cr, sr, cc, sc = jnp.cos(ang_r), jnp.sin(ang_r), jnp.cos(ang_c), jnp.sin(ang_c)
    cos = jnp.concatenate([cr, cr, cc, cc], axis=-1)
    sin = jnp.concatenate([-sr, sr, -sc, sc], axis=-1)

    def rot(g):
        h = ROPE_HALF
        return jnp.concatenate([g[h:2 * h], g[0:h], g[3 * h:4 * h], g[2 * h:3 * h]])

    tabs = [cos * q_gain * Q_SCALE, sin * rot(q_gain) * Q_SCALE,
            cos * k_gain, sin * rot(k_gain)]
    return jnp.stack([tb.T for tb in tabs], axis=0)


def _scores_bounded(q_gain, k_gain):
    bound = HEAD_DIM * jnp.max(jnp.abs(q_gain)) * jnp.max(jnp.abs(k_gain)) * Q_SCALE
    return bound * ATTN_BOUND_MARGIN <= ATTN_SCORE_BOUND


def _trunk(x, tables, p):
    q_t, k, v_t, c = _project(x, p["g_mix"], p["wqkv_t"], p["wc"], tables[:, :, :x.shape[1]])
    attn = _attention(q_t, k, v_t, p["scores_bounded"])
    return _tail(x, attn, c, p["dww"], p["dwb"], p["lng"], p["lnb"], p["wout"], p["g_mlp"],
                 p["wup"], p["wdn"], p["g_fin"])


def kernel(x_prompt, x_sample, norm_mix_g, w_in, q_norm_g, k_norm_g, conv_dw_w, conv_dw_b,
           conv_ln_g, conv_ln_b, w_out, norm_mlp_g, w_up, w_down, norm_final_g):
    assert w_in.shape[0] == 1, "single-layer trunk"
    w = w_in[0]
    params = dict(
        g_mix=norm_mix_g[0][None, :],
        wqkv_t=w[:, :QKV_W].T.astype(BF16),
        wc=w[:, QKV_W:].astype(BF16),
        dww=conv_dw_w[0], dwb=conv_dw_b[0][None, :],
        lng=conv_ln_g[0][None, :], lnb=conv_ln_b[0][None, :],
        wout=w_out[0].astype(BF16), g_mlp=norm_mlp_g[0][None, :],
        wup=w_up[0].astype(BF16), wdn=w_down[0].astype(BF16),
        g_fin=norm_final_g[None, :],
        scores_bounded=_scores_bounded(q_norm_g[0], k_norm_g[0]),
    )
    max_s = max(x_prompt.shape[1], x_sample.shape[1])
    tables = _rope_tables(max_s, q_norm_g[0], k_norm_g[0])
    return (_trunk(x_prompt, tables, params), _trunk(x_sample, tables, params))
```

```python
import functools
import math

import jax
import jax.numpy as jnp
from jax import lax
from jax.experimental import pallas as pl
from jax.experimental.pallas import tpu as pltpu

F32 = jnp.float32
BF16 = jnp.bfloat16

D_MODEL = 1024
GRID_W = 64
HEAD_DIM = 64
N_Q_HEADS = 8
N_KV_HEADS = 2
GQA_GROUP = N_Q_HEADS // N_KV_HEADS
Q_W = N_Q_HEADS * HEAD_DIM
KV_W = N_KV_HEADS * HEAD_DIM
QKV_W = Q_W + 2 * KV_W
AXIS_DIM = HEAD_DIM // 2
ROPE_HALF = AXIS_DIM // 2
ROPE_THETA = 10000.0
CONV_W = D_MODEL // 2
CONV_K = 31
CONV_PAD = CONV_K // 2
D_FF = 4 * D_MODEL
EPS = 1e-6
LN_EPS = 1e-5

V7X_VMEM_BYTES = 64 * 1024 * 1024
LANES = 128
SUBLANES = 8
LANE_TILES = CONV_W // LANES

TOKEN_TILE = 512
HALO = 2 * SUBLANES
CONV_TILE = 1024
CONV_ROWS = 128
FF_CHUNK = 1024
ATTN_Q_SPLIT = 2
ATTN_AHEAD = 3
ATTN_S_SLOTS = 4
ATTN_MAX_UNROLL = 8
ATTN_BOUNDED_AHEAD = 2
ATTN_P_SLOTS = 4
ATTN_SCORE_BOUND = 40.0
ATTN_BOUND_MARGIN = 1.02
Q_SCALE = math.log2(math.e) / math.sqrt(HEAD_DIM)
ONES_ROWS = 16
NEG_BIG = -1e30

assert HALO >= CONV_PAD and CONV_TILE % HALO == 0 and CONV_TILE % CONV_ROWS == 0


def _vmem_limit(nbytes):
    return int(min(nbytes, V7X_VMEM_BYTES - 6 * 1024 * 1024))


def _proj_kernel(x_ref, g_ref, wqkv_t_ref, wc_ref, tab_ref, q_t_ref, k_ref, v_t_ref, c_ref):
    x = x_ref[0]
    ms = jnp.mean(x * x, axis=-1, keepdims=True)
    xn = (x * lax.rsqrt(ms + EPS) * g_ref[...]).astype(BF16)

    zc = jnp.dot(xn, wc_ref[...], preferred_element_type=F32)
    cv = zc[:, :CONV_W]
    cg = zc[:, CONV_W:]
    c_ref[0] = cv * (1.0 / (1.0 + jnp.exp(-cg)))

    z_t = lax.dot_general(wqkv_t_ref[...], xn, (((1,), (1,)), ((), ())),
                          preferred_element_type=F32)

    def norm_rope(z, a, b):
        r = lax.rsqrt(jnp.mean(z * z, axis=0, keepdims=True) + EPS)
        h = ROPE_HALF
        rot = jnp.concatenate([z[h:2 * h], z[0:h], z[3 * h:4 * h], z[2 * h:3 * h]], axis=0)
        return (z * a + rot * b) * r

    for h in range(N_Q_HEADS):
        zq = z_t[h * HEAD_DIM:(h + 1) * HEAD_DIM]
        q_t_ref[0, h, 0] = norm_rope(zq, tab_ref[0], tab_ref[1]).astype(BF16)
    ks = []
    for h in range(N_KV_HEADS):
        zk = z_t[Q_W + h * HEAD_DIM:Q_W + (h + 1) * HEAD_DIM]
        ks.append(norm_rope(zk, tab_ref[2], tab_ref[3]))
        zv = z_t[Q_W + KV_W + h * HEAD_DIM:Q_W + KV_W + (h + 1) * HEAD_DIM]
        v_t_ref[0, h, 0] = zv.astype(BF16)
    k_nat = jnp.concatenate(ks, axis=0).T
    for h in range(N_KV_HEADS):
        k_ref[0, h] = k_nat[:, h * HEAD_DIM:(h + 1) * HEAD_DIM].astype(BF16)


def _project(x, g, wqkv_t, wc, tables):
    b, s, _ = x.shape
    t = TOKEN_TILE
    n = s // t
    const = lambda *_: (0, 0)
    return pl.pallas_call(
        _proj_kernel,
        grid=(b, n),
        in_specs=[
            pl.BlockSpec((1, t, D_MODEL), lambda i, j: (i, j, 0)),
            pl.BlockSpec((1, D_MODEL), const),
            pl.BlockSpec((QKV_W, D_MODEL), const),
            pl.BlockSpec((D_MODEL, 2 * CONV_W), const),
            pl.BlockSpec((4, HEAD_DIM, t), lambda i, j: (0, 0, j)),
        ],
        out_specs=[
            pl.BlockSpec((1, N_Q_HEADS, 1, HEAD_DIM, t), lambda i, j: (i, 0, j, 0, 0)),
            pl.BlockSpec((1, N_KV_HEADS, t, HEAD_DIM), lambda i, j: (i, 0, j, 0)),
            pl.BlockSpec((1, N_KV_HEADS, 1, HEAD_DIM, t), lambda i, j: (i, 0, j, 0, 0)),
            pl.BlockSpec((1, t, CONV_W), lambda i, j: (i, j, 0)),
        ],
        out_shape=[
            jax.ShapeDtypeStruct((b, N_Q_HEADS, n, HEAD_DIM, t), BF16),
            jax.ShapeDtypeStruct((b, N_KV_HEADS, s, HEAD_DIM), BF16),
            jax.ShapeDtypeStruct((b, N_KV_HEADS, n, HEAD_DIM, t), BF16),
            jax.ShapeDtypeStruct((b, s, CONV_W), F32),
        ],
        compiler_params=pltpu.CompilerParams(
            dimension_semantics=("arbitrary", "arbitrary"),
            vmem_limit_bytes=_vmem_limit(48 * 1024 * 1024)),
        name="proj",
    )(x, g, wqkv_t, wc, tables)


def _attn_kernel(q_t_ref, k_ref, v_t_ref, o_ref, s_ref, p_ref, m_ref, a_ref, acc_ref, *,
                 n_chunks, unroll):
    t = TOKEN_TILE
    tq = t // ATTN_Q_SPLIT
    nu = GQA_GROUP * ATTN_Q_SPLIT
    ones = jnp.ones((ONES_ROWS, t), BF16)
    m_ref[...] = jnp.full(m_ref.shape, NEG_BIG, F32)
    acc_ref[...] = jnp.zeros(acc_ref.shape, F32)
    p_ref[(nu - 1) % 2] = jnp.zeros((t, tq), BF16)
    a_ref[(nu - 1) % 2] = jnp.ones((1, tq), F32)

    def scores(c, u):
        h, half = divmod(u, ATTN_Q_SPLIT)
        k = k_ref[0, 0, pl.ds(pl.multiple_of(c * t, t), t), :]
        q_t = q_t_ref[0, h, 0, :, half * tq:(half + 1) * tq]
        s_ref[u % ATTN_S_SLOTS] = jnp.dot(k, q_t, preferred_element_type=F32)

    def softmax(u):
        s = s_ref[u % ATTN_S_SLOTS]
        m_old = m_ref[u]
        m_new = jnp.maximum(m_old, jnp.max(s, axis=0, keepdims=True))
        a_ref[u % 2] = jnp.exp2(m_old - m_new)
        m_ref[u] = m_new
        p_ref[u % 2] = jnp.exp2(s - m_new).astype(BF16)

    def values(c, u):
        v_aug = jnp.concatenate([v_t_ref[0, 0, c], ones], axis=0)
        acc_ref[u] = a_ref[u % 2] * acc_ref[u] + jnp.dot(v_aug, p_ref[u % 2],
                                                         preferred_element_type=F32)

    for u in range(ATTN_AHEAD):
        scores(0, u)

    def body(cc, carry):
        for i in range(unroll):
            c = cc * unroll + i
            for u in range(nu):
                if u + ATTN_AHEAD < nu:
                    scores(c, u + ATTN_AHEAD)
                else:
                    scores(jnp.minimum(c + 1, n_chunks - 1), u + ATTN_AHEAD - nu)
                softmax(u)
                if u > 0:
                    values(c, u - 1)
                else:
                    values(jnp.maximum(c - 1, 0), nu - 1)
        return carry

    lax.fori_loop(0, n_chunks // unroll, body, 0)
    values(n_chunks - 1, nu - 1)
    outs = []
    for h in range(GQA_GROUP):
        halves = [acc_ref[h * ATTN_Q_SPLIT + i, :HEAD_DIM]
                  / acc_ref[h * ATTN_Q_SPLIT + i, HEAD_DIM:HEAD_DIM + 1]
                  for i in range(ATTN_Q_SPLIT)]
        outs.append(jnp.concatenate(halves, axis=1))
    o_ref[0] = jnp.concatenate(outs, axis=0).T.astype(BF16)


def _attn_bounded_kernel(q_t_ref, k_ref, v_t_ref, o_ref, p_ref, acc_ref, *, n_chunks, unroll):
    t = TOKEN_TILE
    tq = t // ATTN_Q_SPLIT
    nu = GQA_GROUP * ATTN_Q_SPLIT
    ones = jnp.ones((ONES_ROWS, t), BF16)
    acc_ref[...] = jnp.zeros(acc_ref.shape, F32)
    p_ref[(nu - 1) % ATTN_P_SLOTS] = jnp.zeros((t, tq), BF16)

    def probs(c, u):
        h, half = divmod(u, ATTN_Q_SPLIT)
        k = k_ref[0, 0, pl.ds(pl.multiple_of(c * t, t), t), :]
        q_t = q_t_ref[0, h, 0, :, half * tq:(half + 1) * tq]
        s = jnp.dot(k, q_t, preferred_element_type=F32)
        p_ref[u % ATTN_P_SLOTS] = jnp.exp2(s).astype(BF16)

    def values(c, u):
        v_aug = jnp.concatenate([v_t_ref[0, 0, c], ones], axis=0)
        acc_ref[u] = acc_ref[u] + jnp.dot(v_aug, p_ref[u % ATTN_P_SLOTS],
                                          preferred_element_type=F32)

    for u in range(ATTN_BOUNDED_AHEAD):
        probs(0, u)

    def body(cc, carry):
        for i in range(unroll):
            c = cc * unroll + i
            for u in range(nu):
                if u + ATTN_BOUNDED_AHEAD < nu:
                    probs(c, u + ATTN_BOUNDED_AHEAD)
                else:
                    probs(jnp.minimum(c + 1, n_chunks - 1), u + ATTN_BOUNDED_AHEAD - nu)
                if u > 0:
                    values(c, u - 1)
                else:
                    values(jnp.maximum(c - 1, 0), nu - 1)
        return carry

    lax.fori_loop(0, n_chunks // unroll, body, 0)
    values(n_chunks - 1, nu - 1)
    outs = []
    for h in range(GQA_GROUP):
        halves = [acc_ref[h * ATTN_Q_SPLIT + i, :HEAD_DIM]
                  / acc_ref[h * ATTN_Q_SPLIT + i, HEAD_DIM:HEAD_DIM + 1]
                  for i in range(ATTN_Q_SPLIT)]
        outs.append(jnp.concatenate(halves, axis=1))
    o_ref[0] = jnp.concatenate(outs, axis=0).T.astype(BF16)


def _attention_call(q_t, k, v_t, bounded):
    b, _, n, _, t = q_t.shape
    s = n * t
    gw = GQA_GROUP * HEAD_DIM
    tq = t // ATTN_Q_SPLIT
    nu = GQA_GROUP * ATTN_Q_SPLIT
    unroll = math.gcd(n, ATTN_MAX_UNROLL)
    assert nu % 2 == 0 and nu % ATTN_S_SLOTS == 0 and ATTN_AHEAD < ATTN_S_SLOTS
    assert nu % ATTN_P_SLOTS == 0 and ATTN_BOUNDED_AHEAD + 1 < ATTN_P_SLOTS
    if bounded:
        body = functools.partial(_attn_bounded_kernel, n_chunks=n, unroll=unroll)
        scratch = [
            pltpu.VMEM((ATTN_P_SLOTS, t, tq), BF16),
            pltpu.VMEM((nu, HEAD_DIM + ONES_ROWS, tq), F32),
        ]
    else:
        body = functools.partial(_attn_kernel, n_chunks=n, unroll=unroll)
        scratch = [
            pltpu.VMEM((ATTN_S_SLOTS, t, tq), F32),
            pltpu.VMEM((2, t, tq), BF16),
            pltpu.VMEM((nu, 1, tq), F32),
            pltpu.VMEM((2, 1, tq), F32),
            pltpu.VMEM((nu, HEAD_DIM + ONES_ROWS, tq), F32),
        ]
    return pl.pallas_call(
        body,
        grid=(b, N_KV_HEADS, n),
        in_specs=[
            pl.BlockSpec((1, GQA_GROUP, 1, HEAD_DIM, t), lambda i, g, j: (i, g, j, 0, 0)),
            pl.BlockSpec((1, 1, s, HEAD_DIM), lambda i, g, j: (i, g, 0, 0)),
            pl.BlockSpec((1, 1, n, HEAD_DIM, t), lambda i, g, j: (i, g, 0, 0, 0)),
        ],
        out_specs=pl.BlockSpec((1, t, gw), lambda i, g, j: (i, j, g)),
        out_shape=jax.ShapeDtypeStruct((b, s, Q_W), BF16),
        scratch_shapes=scratch,
        compiler_params=pltpu.CompilerParams(
            dimension_semantics=("arbitrary", "arbitrary", "arbitrary"),
            vmem_limit_bytes=_vmem_limit(48 * 1024 * 1024)),
        name="attn_bounded" if bounded else "attn",
    )(q_t, k, v_t)


def _attention(q_t, k, v_t, scores_bounded):
    return lax.cond(scores_bounded,
                    functools.partial(_attention_call, bounded=True),
                    functools.partial(_attention_call, bounded=False),
                    q_t, k, v_t)


def _conv_kernel(c_ref, cp_ref, cn_ref, dww_ref, dwb_ref, lng_ref, lnb_ref, act_ref,
                 cext_ref, conv_ref):
    t = CONV_TILE
    j = pl.program_id(1)
    has_prev = (j > 0).astype(F32)
    has_next = (j < pl.num_programs(1) - 1).astype(F32)
    for q in range(LANE_TILES):
        lanes = slice(q * LANES, (q + 1) * LANES)
        rows = lambda tok, n, q=q: pl.ds(tok * LANE_TILES + q, n, stride=LANE_TILES)
        cext_ref[rows(0, HALO), :] = cp_ref[0, :, lanes] * has_prev
        cext_ref[rows(HALO, t), :] = c_ref[0, :, lanes]
        cext_ref[rows(HALO + t, HALO), :] = cn_ref[0, :, lanes] * has_next

    nblk = CONV_ROWS // SUBLANES
    for q in range(LANE_TILES):
        lanes = slice(q * LANES, (q + 1) * LANES)
        taps = [jnp.broadcast_to(dww_ref[tap:tap + 1, lanes], (SUBLANES, LANES))
                for tap in range(CONV_K)]
        bias = jnp.broadcast_to(dwb_ref[:, lanes], (SUBLANES, LANES))

        def conv_rows(r, carry, q=q, lanes=lanes, taps=taps, bias=bias):
            base = pl.multiple_of(r * CONV_ROWS, CONV_ROWS)
            acc = [bias] * nblk
            for w in range(CONV_ROWS - SUBLANES + CONV_K):
                row = (base + w + HALO - CONV_PAD) * LANE_TILES + q
                x = cext_ref[pl.ds(row, SUBLANES, stride=LANE_TILES), :]
                for blk in range(nblk):
                    tap = w - blk * SUBLANES
                    if 0 <= tap < CONV_K:
                        acc[blk] = acc[blk] + taps[tap] * x
            conv_ref[pl.ds(base, CONV_ROWS), lanes] = jnp.concatenate(acc, axis=0)
            return carry

        lax.fori_loop(0, t // CONV_ROWS, conv_rows, 0)

    cc = conv_ref[...]
    mu = jnp.mean(cc, axis=-1, keepdims=True)
    xc = cc - mu
    var = jnp.mean(xc * xc, axis=-1, keepdims=True)
    ln = xc * lax.rsqrt(var + LN_EPS) * lng_ref[...] + lnb_ref[...]
    act_ref[0] = (ln * (1.0 / (1.0 + jnp.exp(-ln)))).astype(BF16)


def _conv_branch(c, dww, dwb, lng, lnb):
    b, s, _ = c.shape
    t = CONV_TILE
    n = s // t
    hb = t // HALO
    last = s // HALO - 1
    const = lambda *_: (0, 0)
    return pl.pallas_call(
        _conv_kernel,
        grid=(b, n),
        in_specs=[
            pl.BlockSpec((1, t, CONV_W), lambda i, j: (i, j, 0)),
            pl.BlockSpec((1, HALO, CONV_W), lambda i, j: (i, jnp.maximum(j * hb - 1, 0), 0)),
            pl.BlockSpec((1, HALO, CONV_W), lambda i, j: (i, jnp.minimum((j + 1) * hb, last), 0)),
            pl.BlockSpec((CONV_K, CONV_W), const),
            pl.BlockSpec((1, CONV_W), const),
            pl.BlockSpec((1, CONV_W), const),
            pl.BlockSpec((1, CONV_W), const),
        ],
        out_specs=pl.BlockSpec((1, t, CONV_W), lambda i, j: (i, j, 0)),
        out_shape=jax.ShapeDtypeStruct((b, s, CONV_W), BF16),
        scratch_shapes=[
            pltpu.VMEM(((t + 2 * HALO) * LANE_TILES, LANES), F32),
            pltpu.VMEM((t, CONV_W), F32),
        ],
        compiler_params=pltpu.CompilerParams(
            dimension_semantics=("arbitrary", "arbitrary"),
            vmem_limit_bytes=_vmem_limit(48 * 1024 * 1024)),
        name="conv",
    )(c, c, c, dww, dwb, lng, lnb)


def _tail_kernel(x_ref, a_ref, act_ref, wout_ref, gm_ref, wup_ref, wdn_ref, gf_ref, y_ref):
    t = TOKEN_TILE
    mix = jnp.dot(a_ref[0], wout_ref[:Q_W, :], preferred_element_type=F32)
    mix = mix + jnp.dot(act_ref[0], wout_ref[Q_W:, :], preferred_element_type=F32)
    h = x_ref[0] + mix

    hn = h * lax.rsqrt(jnp.mean(h * h, axis=-1, keepdims=True) + EPS) * gm_ref[...]
    hn = hn.astype(BF16)
    mlp = jnp.zeros((t, D_MODEL), F32)
    for f in range(D_FF // FF_CHUNK):
        u = jnp.dot(hn, wup_ref[:, f * FF_CHUNK:(f + 1) * FF_CHUNK],
                    preferred_element_type=F32)
        u = jnp.maximum(u, 0.0)
        mlp = mlp + jnp.dot((u * u).astype(BF16), wdn_ref[f * FF_CHUNK:(f + 1) * FF_CHUNK, :],
                            preferred_element_type=F32)
    h2 = h + mlp
    y_ref[0] = h2 * lax.rsqrt(jnp.mean(h2 * h2, axis=-1, keepdims=True) + EPS) * gf_ref[...]


def _tail(x, attn, act, wout, gm, wup, wdn, gf):
    b, s, _ = x.shape
    t = TOKEN_TILE
    n = s // t
    const = lambda *_: (0, 0)
    resident = dict(pipeline_mode=pl.Buffered(1))
    return pl.pallas_call(
        _tail_kernel,
        grid=(b, n),
        in_specs=[
            pl.BlockSpec((1, t, D_MODEL), lambda i, j: (i, j, 0)),
            pl.BlockSpec((1, t, Q_W), lambda i, j: (i, j, 0)),
            pl.BlockSpec((1, t, CONV_W), lambda i, j: (i, j, 0)),
            pl.BlockSpec((Q_W + CONV_W, D_MODEL), const, **resident),
            pl.BlockSpec((1, D_MODEL), const),
            pl.BlockSpec((D_MODEL, D_FF), const, **resident),
            pl.BlockSpec((D_FF, D_MODEL), const, **resident),
            pl.BlockSpec((1, D_MODEL), const),
        ],
        out_specs=pl.BlockSpec((1, t, D_MODEL), lambda i, j: (i, j, 0)),
        out_shape=jax.ShapeDtypeStruct((b, s, D_MODEL), F32),
        compiler_params=pltpu.CompilerParams(
            dimension_semantics=("arbitrary", "arbitrary"),
            vmem_limit_bytes=_vmem_limit(V7X_VMEM_BYTES)),
        name="tail",
    )(x, attn, act, wout, gm, wup, wdn, gf)


def _rope_tables(seq_len, q_gain, k_gain):
    pos = jnp.arange(seq_len, dtype=jnp.int32)
    row = (pos // GRID_W).astype(F32)
    col = (pos % GRID_W).astype(F32)
    inv_freq = ROPE_THETA ** (-jnp.arange(0, AXIS_DIM, 2, dtype=F32) / AXIS_DIM)
    ang_r = row[:, None] * inv_freq[None, :]
    ang_c = col[:, None] * inv_freq[None, :]
    cr, sr, cc, sc = jnp.cos(ang_r), jnp.sin(ang_r), jnp.cos(ang_c), jnp.sin(ang_c)
    cos = jnp.concatenate([cr, cr, cc, cc], axis=-1)
    sin = jnp.concatenate([-sr, sr, -sc, sc], axis=-1)

    def rot(g):
        h = ROPE_HALF
        return jnp.concatenate([g[h:2 * h], g[0:h], g[3 * h:4 * h], g[2 * h:3 * h]])

    tabs = [cos * q_gain * Q_SCALE, sin * rot(q_gain) * Q_SCALE,
            cos * k_gain, sin * rot(k_gain)]
    return jnp.stack([tb.T for tb in tabs], axis=0)


def _scores_bounded(q_gain, k_gain):
    bound = HEAD_DIM * jnp.max(jnp.abs(q_gain)) * jnp.max(jnp.abs(k_gain)) * Q_SCALE
    return bound * ATTN_BOUND_MARGIN <= ATTN_SCORE_BOUND


def _trunk(x, tables, p):
    q_t, k, v_t, c = _project(x, p["g_mix"], p["wqkv_t"], p["wc"], tables[:, :, :x.shape[1]])
    attn = _attention(q_t, k, v_t, p["scores_bounded"])
    act = _conv_branch(c, p["dww"], p["dwb"], p["lng"], p["lnb"])
    return _tail(x, attn, act, p["wout"], p["g_mlp"], p["wup"], p["wdn"], p["g_fin"])


def kernel(x_prompt, x_sample, norm_mix_g, w_in, q_norm_g, k_norm_g, conv_dw_w, conv_dw_b,
           conv_ln_g, conv_ln_b, w_out, norm_mlp_g, w_up, w_down, norm_final_g):
    assert w_in.shape[0] == 1, "single-layer trunk"
    w = w_in[0]
    params = dict(
        g_mix=norm_mix_g[0][None, :],
        wqkv_t=w[:, :QKV_W].T.astype(BF16),
        wc=w[:, QKV_W:].astype(BF16),
        dww=conv_dw_w[0], dwb=conv_dw_b[0][None, :],
        lng=conv_ln_g[0][None, :], lnb=conv_ln_b[0][None, :],
        wout=w_out[0].astype(BF16), g_mlp=norm_mlp_g[0][None, :],
        wup=w_up[0].astype(BF16), wdn=w_down[0].astype(BF16),
        g_fin=norm_final_g[None, :],
        scores_bounded=_scores_bounded(q_norm_g[0], k_norm_g[0]),
    )
    max_s = max(x_prompt.shape[1], x_sample.shape[1])
    tables = _rope_tables(max_s, q_norm_g[0], k_norm_g[0])
    return (_trunk(x_prompt, tables, params), _trunk(x_sample, tables, params))
```
